```python
import jax, jax.numpy as jnp
from jax import lax
import numpy as np

D_MODEL = 1024
BATCH = 2
SEQ = 16384
DEPTH = 2
DEC_BATCH = 16
DEC_SEQ = 4096
PAST_LEN = 128

N_MIXERS = 2
HEAD_DIM = 64
A_HEADS = 16
A_KV_HEADS = 4
A_WINDOW = 128
A_QKV = (A_HEADS + 2 * A_KV_HEADS) * HEAD_DIM
B_GROUPS = ((128, 1), (512, 4), (2048, 16))
B_HEADS = 8
B_WIDTH = len(B_GROUPS) * B_HEADS * HEAD_DIM
N_EXPERTS = 16
EXPERT_FF = 1024
EC_CAPACITY = 2
ROPE_THETA = 10000.0
EPS = 1e-6
NEG = -1e30
N_A_LAYERS = (DEPTH + 1) // 2
N_B_LAYERS = DEPTH // 2

kernel_name = "hybrid_bidir_window_dilated_ec_moe"


def rms_norm(x, g):
    xf = x.astype(jnp.float32)
    y = xf * lax.rsqrt(jnp.mean(xf * xf, axis=-1, keepdims=True) + EPS)
    return (y * g.astype(jnp.float32)).astype(x.dtype)


def modulate(h, shift, scale):
    return h * (1 + scale[:, None, :]) + shift[:, None, :]


def rope(x, pos):
    half = x.shape[-1] // 2
    inv = ROPE_THETA ** (-jnp.arange(half, dtype=jnp.float32) / half)
    ang = pos.astype(jnp.float32)[:, None] * inv[None, :]
    cos = jnp.cos(ang)[None, :, None, :]
    sin = jnp.sin(ang)[None, :, None, :]
    xf = x.astype(jnp.float32)
    x1, x2 = xf[..., :half], xf[..., half:]
    return jnp.concatenate([x1 * cos - x2 * sin, x2 * cos + x1 * sin], axis=-1).astype(x.dtype)


def banded_attention(q, k, v, window, sink=None):
    bsz, L, H, hd = q.shape
    hkv = k.shape[2]
    rep = H // hkv
    blk = window
    nb = -(-L // blk)
    Lp = nb * blk
    out_dtype = q.dtype
    qp = jnp.pad(q, ((0, 0), (0, Lp - L), (0, 0), (0, 0)))
    kv_pad = ((0, 0), (blk, Lp - L + blk), (0, 0), (0, 0))
    kp = jnp.pad(k, kv_pad)
    vp = jnp.pad(v, kv_pad)
    qb = jnp.moveaxis(qp.reshape(bsz, nb, blk, hkv, rep, hd), 1, 0)
    scale = hd ** -0.5

    def block(args):
        j, qj = args
        kj = lax.dynamic_slice_in_dim(kp, j * blk, 3 * blk, axis=1)
        vj = lax.dynamic_slice_in_dim(vp, j * blk, 3 * blk, axis=1)
        qpos = j * blk + jnp.arange(blk)
        kpos = (j - 1) * blk + jnp.arange(3 * blk)
        valid = ((jnp.abs(qpos[:, None] - kpos[None, :]) <= window)
                 & (kpos >= 0)[None, :] & (kpos < L)[None, :])
        s = jnp.einsum('bqgrd,bkgd->bgrqk', qj, kj).astype(jnp.float32) * scale
        s = jnp.where(valid, s, NEG)
        m = jnp.max(s, axis=-1)
        if sink is not None:
            sk = sink.astype(jnp.float32).reshape(hkv, rep)[None, :, :, None]
            m = jnp.maximum(m, sk)
        p = jnp.exp(s - m[..., None])
        l = jnp.sum(p, axis=-1)
        if sink is not None:
            l = l + jnp.exp(sk - m)
        o = jnp.einsum('bgrqk,bkgd->bqgrd', p, vj.astype(jnp.float32))
        o = o / jnp.transpose(l, (0, 3, 1, 2))[..., None]
        lse = jnp.transpose(m + jnp.log(l), (0, 3, 1, 2))
        return o.astype(out_dtype), lse

    o, lse = lax.map(block, (jnp.arange(nb), qb))
    o = jnp.moveaxis(o, 0, 1).reshape(bsz, Lp, H, hd)[:, :L]
    lse = jnp.moveaxis(lse, 0, 1).reshape(bsz, Lp, H)[:, :L]
    return o, lse


def mixer_a(h, w_qkv, sink, w_o):
    bsz, S, _ = h.shape
    qkv = h @ w_qkv
    q, k, v = jnp.split(qkv, [A_HEADS * HEAD_DIM, (A_HEADS + A_KV_HEADS) * HEAD_DIM], axis=-1)
    q = q.reshape(bsz, S, A_HEADS, HEAD_DIM)
    k = k.reshape(bsz, S, A_KV_HEADS, HEAD_DIM)
    v = v.reshape(bsz, S, A_KV_HEADS, HEAD_DIM)
    pos = jnp.arange(S)
    q = rope(q, pos)
    k = rope(k, pos)
    o, _ = banded_attention(q, k, v, A_WINDOW, sink)
    return o.reshape(bsz, S, A_HEADS * HEAD_DIM) @ w_o


def dilated_group(q, k, v, window, dil):
    bsz, S, H, hd = q.shape
    n = S // dil

    def to_strided(t):
        return t.reshape(bsz, n, dil, H, hd).transpose(0, 2, 1, 3, 4).reshape(bsz * dil, n, H, hd)

    o, lse = banded_attention(to_strided(q), to_strided(k), to_strided(v), (window // 2) // dil)
    o = o.reshape(bsz, dil, n, H, hd).transpose(0, 2, 1, 3, 4).reshape(bsz, S, H, hd)
    lse = lse.reshape(bsz, dil, n, H).transpose(0, 2, 1, 3).reshape(bsz, S, H)
    return o, lse


def mixer_b(h, w_qkv, w_o):
    bsz, S, _ = h.shape
    G = len(B_GROUPS)
    qkv = h @ w_qkv
    q, k, v = jnp.split(qkv, 3, axis=-1)
    pos = jnp.arange(S)
    q = rope(q.reshape(bsz, S, G * B_HEADS, HEAD_DIM), pos).reshape(bsz, S, G, B_HEADS, HEAD_DIM)
    k = rope(k.reshape(bsz, S, G * B_HEADS, HEAD_DIM), pos).reshape(bsz, S, G, B_HEADS, HEAD_DIM)
    v = v.reshape(bsz, S, G, B_HEADS, HEAD_DIM)
    outs, lses = [], []
    for g, (window, dil) in enumerate(B_GROUPS):
        o_g, lse_g = dilated_group(q[:, :, g], k[:, :, g], v[:, :, g], window, dil)
        outs.append(o_g)
        lses.append(lse_g)
    o = jnp.stack(outs, axis=0).astype(jnp.float32)
    alpha = jax.nn.softmax(jnp.stack(lses, axis=0), axis=0)
    o = jnp.sum(alpha[..., None] * o, axis=0).astype(h.dtype)
    return o.reshape(bsz, S, B_HEADS * HEAD_DIM) @ w_o


def expert_choice_moe(h, w_router, b_router, w_gate, w_up, w_down):
    bsz, S, D = h.shape
    n = bsz * S
    cap = max(1, EC_CAPACITY * n // N_EXPERTS)
    t = h.reshape(n, D)
    aff = jax.nn.softmax((t @ w_router).astype(jnp.float32) + b_router.astype(jnp.float32), axis=-1)
    gate, idx = lax.top_k(aff.T, cap)
    xe = t[idx]
    hid = jax.nn.silu(jnp.einsum('ecd,edf->ecf', xe, w_gate)) * jnp.einsum('ecd,edf->ecf', xe, w_up)
    ye = jnp.einsum('ecf,efd->ecd', hid, w_down) * gate[..., None].astype(h.dtype)
    out = jnp.zeros_like(t).at[idx.reshape(-1)].add(ye.reshape(-1, D))
    return out.reshape(bsz, S, D)


def trunk(x, c, w_ada, b_ada, g_mix, g_ffn, w_qkv_a, sink_a, w_o_a, w_qkv_b, w_o_b,
          w_router, b_router, w_gate, w_up, w_down, g_final):
    for i in range(DEPTH):
        mod = jax.nn.silu(c) @ w_ada[i] + b_ada[i]
        sh1, sc1, gt1, sh2, sc2, gt2 = jnp.split(mod, 6, axis=-1)
        h = modulate(rms_norm(x, g_mix[i]), sh1, sc1)
        if i % N_MIXERS == 0:
            j = i // N_MIXERS
            y = mixer_a(h, w_qkv_a[j], sink_a[j], w_o_a[j])
        else:
            j = i // N_MIXERS
            y = mixer_b(h, w_qkv_b[j], w_o_b[j])
        x = x + gt1[:, None, :] * y
        h = modulate(rms_norm(x, g_ffn[i]), sh2, sc2)
        x = x + gt2[:, None, :] * expert_choice_moe(h, w_router[i], b_router[i], w_gate[i], w_up[i], w_down[i])
    return rms_norm(x, g_final)


def setup_inputs(seed: int = 0) -> dict:
    key = jax.random.key(seed)
    ks = jax.random.split(key, 20)
    f32 = jnp.float32
    D = D_MODEL

    def nrm(k, shape, scale):
        return jax.random.normal(k, shape, f32) * scale

    gate_offset = jnp.zeros((6, D), f32).at[2].set(1.0).at[5].set(1.0).reshape(6 * D)
    return {
        "x_prompt": nrm(ks[0], (BATCH, SEQ, D), 1.0),
        "x_sample": nrm(ks[1], (DEC_BATCH, DEC_SEQ, D), 1.0),
        "c_prompt": nrm(ks[2], (BATCH, D), 1.0),
        "c_sample": nrm(ks[3], (DEC_BATCH, D), 1.0),
        "w_ada": nrm(ks[4], (DEPTH, D, 6 * D), 0.1 * D ** -0.5),
        "b_ada": nrm(ks[5], (DEPTH, 6 * D), 0.02) + gate_offset,
        "g_mix": 1.0 + nrm(ks[6], (DEPTH, D), 0.02),
        "g_ffn": 1.0 + nrm(ks[7], (DEPTH, D), 0.02),
        "w_qkv_a": nrm(ks[8], (N_A_LAYERS, D, A_QKV), D ** -0.5),
        "sink_a": nrm(ks[9], (N_A_LAYERS, A_HEADS), 0.5),
        "w_o_a": nrm(ks[10], (N_A_LAYERS, A_HEADS * HEAD_DIM, D), (A_HEADS * HEAD_DIM) ** -0.5),
        "w_qkv_b": nrm(ks[11], (N_B_LAYERS, D, 3 * B_WIDTH), D ** -0.5),
        "w_o_b": nrm(ks[12], (N_B_LAYERS, B_HEADS * HEAD_DIM, D), (B_HEADS * HEAD_DIM) ** -0.5),
        "w_router": nrm(ks[13], (DEPTH, D, N_EXPERTS), D ** -0.5),
        "b_router": nrm(ks[14], (DEPTH, N_EXPERTS), 0.01),
        "w_gate": nrm(ks[15], (DEPTH, N_EXPERTS, D, EXPERT_FF), D ** -0.5),
        "w_up": nrm(ks[16], (DEPTH, N_EXPERTS, D, EXPERT_FF), D ** -0.5),
        "w_down": nrm(ks[17], (DEPTH, N_EXPERTS, EXPERT_FF, D), EXPERT_FF ** -0.5),
        "g_final": 1.0 + nrm(ks[18], (D,), 0.02),
    }


def reference(x_prompt, x_sample, c_prompt, c_sample, w_ada, b_ada, g_mix, g_ffn,
              w_qkv_a, sink_a, w_o_a, w_qkv_b, w_o_b, w_router, b_router,
              w_gate, w_up, w_down, g_final):
    y_prompt = trunk(x_prompt, c_prompt, w_ada, b_ada, g_mix, g_ffn, w_qkv_a, sink_a, w_o_a,
                     w_qkv_b, w_o_b, w_router, b_router, w_gate, w_up, w_down, g_final)
    y_sample = trunk(x_sample, c_sample, w_ada, b_ada, g_mix, g_ffn, w_qkv_a, sink_a, w_o_a,
                     w_qkv_b, w_o_b, w_router, b_router, w_gate, w_up, w_down, g_final)
    return (y_prompt, y_sample)
```

```python
import functools

import jax
import jax.numpy as jnp
from jax import lax
from jax.experimental import pallas as pl
from jax.experimental.pallas import tpu as pltpu

F32 = jnp.float32
BF16 = jnp.bfloat16
I32 = jnp.int32

D_MODEL = 1024
HEAD_DIM = 64
A_HEADS = 16
A_KV_HEADS = 4
A_WINDOW = 128
B_GROUPS = ((128, 1), (512, 4), (2048, 16))
B_HEADS = 8
B_GW = B_HEADS * HEAD_DIM
N_EXPERTS = 16
EXPERT_FF = 1024
EC_CAPACITY = 2
ROPE_THETA = 10000.0
EPS = 1e-6
NEG = -1e30

LANES = 128
SUBLANES = 8
VMEM_LIMIT = 56 * 1024 * 1024

TOK_TILE = 512
ATT_TQ = 256
MOE_TILE = 2048
MOE_CHUNK = 128
MOE_BLK = 128
N_CHUNK = MOE_TILE // MOE_CHUNK


def _cparams(sem):
    return pltpu.CompilerParams(dimension_semantics=sem, vmem_limit_bytes=VMEM_LIMIT)


def _norm_mod(x, g, sh, sc):
    ms = jnp.mean(x * x, axis=-1, keepdims=True)
    y = (x * lax.rsqrt(ms + EPS)) * g
    return y * (1.0 + sc) + sh


def _ada_kernel(c_ref, w_ref, b_ref, o_ref):
    s = jax.nn.silu(c_ref[...])
    o_ref[0] = jnp.dot(s, w_ref[0], preferred_element_type=F32,
                       precision=lax.Precision.HIGHEST) + b_ref[0]


def _ada(c_all, w_ada, b_ada):
    depth, d, n6 = w_ada.shape
    rows = c_all.shape[0]
    tn = 1536
    return pl.pallas_call(
        _ada_kernel,
        out_shape=jax.ShapeDtypeStruct((depth, rows, n6), F32),
        grid=(depth, n6 // tn),
        in_specs=[
            pl.BlockSpec((rows, d), lambda l, j: (0, 0)),
            pl.BlockSpec((1, d, tn), lambda l, j: (l, 0, j)),
            pl.BlockSpec((1, 1, tn), lambda l, j: (l, 0, j)),
        ],
        out_specs=pl.BlockSpec((1, rows, tn), lambda l, j: (l, 0, j)),
        compiler_params=_cparams(("arbitrary", "arbitrary")),
        name="ada",
    )(c_all, w_ada, b_ada.reshape(depth, 1, n6))


def _rope_tables(seq):
    half = HEAD_DIM // 2
    inv = ROPE_THETA ** (-jnp.arange(half, dtype=F32) / half)
    ang = jnp.arange(seq).astype(F32)[:, None] * inv[None, :]
    cos = jnp.cos(ang)
    sin = jnp.sin(ang)
    zero = jnp.zeros_like(sin)
    c = jnp.concatenate([cos, cos, cos, cos], axis=1)
    a = jnp.concatenate([-sin, zero, -sin, zero], axis=1)
    b = jnp.concatenate([zero, sin, zero, sin], axis=1)
    return c, a, b


def _rope_group(xg, c, a, b):
    return xg * c + pltpu.roll(xg, LANES - HEAD_DIM // 2, 1) * a + pltpu.roll(xg, HEAD_DIM // 2, 1) * b


def _qkv_kernel(*refs, n_rope, n_q, has_res):
    if has_res:
        (x_ref, m_ref, gt_ref, g_ref, sh_ref, sc_ref, w_ref, c_ref, a_ref, b_ref,
         xo_ref, q_ref, k_ref, v_ref) = refs
        x = x_ref[...] + gt_ref[0] * m_ref[...]
        xo_ref[...] = x
    else:
        (x_ref, g_ref, sh_ref, sc_ref, w_ref, c_ref, a_ref, b_ref,
         q_ref, k_ref, v_ref) = refs
        x = x_ref[...]
    h = _norm_mod(x, g_ref[...], sh_ref[0], sc_ref[0]).astype(BF16)
    acc = jnp.dot(h, w_ref[...], preferred_element_type=F32)
    c = c_ref[...]
    a = a_ref[...]
    b = b_ref[...]
    scale = HEAD_DIM ** -0.5
    n_k = n_rope - n_q
    for j in range(n_q):
        xg = acc[:, j * LANES:(j + 1) * LANES]
        q_ref[:, j * LANES:(j + 1) * LANES] = (_rope_group(xg, c, a, b) * scale).astype(BF16)
    for j in range(n_k):
        xg = acc[:, (n_q + j) * LANES:(n_q + j + 1) * LANES]
        k_ref[:, j * LANES:(j + 1) * LANES] = _rope_group(xg, c, a, b).astype(BF16)
    v_ref[...] = acc[:, n_rope * LANES:].astype(BF16)


def _qkv(x, res, g, sh, sc, w_bf16, tables, seq, q_cols, k_cols):
    n, d = x.shape
    t = TOK_TILE
    tpb = seq // t
    v_cols = w_bf16.shape[1] - q_cols - k_cols
    c, a, b = tables
    tok = pl.BlockSpec((t, d), lambda i: (i, 0))
    vec = pl.BlockSpec((1, 1, d), lambda i: (i // tpb, 0, 0))
    tab = pl.BlockSpec((t, LANES), lambda i: (i % tpb, 0))
    in_specs = [tok]
    args = [x]
    if res is not None:
        in_specs += [tok, vec]
        args += [res[0], res[1]]
    in_specs += [pl.BlockSpec((1, d), lambda i: (0, 0)), vec, vec,
                 pl.BlockSpec(w_bf16.shape, lambda i: (0, 0)), tab, tab, tab]
    args += [g.reshape(1, d), sh, sc, w_bf16, c, a, b]
    out_shape = [jax.ShapeDtypeStruct((n, q_cols), BF16),
                 jax.ShapeDtypeStruct((n, k_cols), BF16),
                 jax.ShapeDtypeStruct((n, v_cols), BF16)]
    out_specs = [pl.BlockSpec((t, q_cols), lambda i: (i, 0)),
                 pl.BlockSpec((t, k_cols), lambda i: (i, 0)),
                 pl.BlockSpec((t, v_cols), lambda i: (i, 0))]
    if res is not None:
        out_shape = [jax.ShapeDtypeStruct((n, d), F32)] + out_shape
        out_specs = [tok] + out_specs
    kern = functools.partial(_qkv_kernel, n_rope=(q_cols + k_cols) // LANES,
                             n_q=q_cols // LANES, has_res=res is not None)
    return pl.pallas_call(
        kern, out_shape=out_shape, grid=(n // t,), in_specs=in_specs, out_specs=out_specs,
        compiler_params=_cparams(("arbitrary",)), name="qkv",
    )(*args)


def _attn_kernel(*refs, tq, win, n_kv, rep, seq_len, has_sink, want_lse):
    pos = 0
    if has_sink:
        sink_ref = refs[0]
        pos = 1
    q_ref, kp_ref, kc_ref, kn_ref, vp_ref, vc_ref, vn_ref = refs[pos:pos + 7]
    o_ref = refs[pos + 7]
    lse_ref = refs[pos + 8] if want_lse else None
    i = pl.program_id(2)
    tk = tq + 2 * win
    q = q_ref[0]
    k = jnp.concatenate([kp_ref[0], kc_ref[0], kn_ref[0]], axis=0)
    v = jnp.concatenate([vp_ref[0], vc_ref[0], vn_ref[0]], axis=0)
    qpos = i * tq + lax.broadcasted_iota(I32, (tq, tk), 0)
    kpos = i * tq - win + lax.broadcasted_iota(I32, (tq, tk), 1)
    valid = (jnp.abs(qpos - kpos) <= win) & (kpos >= 0) & (kpos < seq_len)
    for g in range(n_kv):
        kg = k[:, g * HEAD_DIM:(g + 1) * HEAD_DIM]
        vg = v[:, g * HEAD_DIM:(g + 1) * HEAD_DIM]
        for r in range(rep):
            hh = g * rep + r
            qh = q[:, hh * HEAD_DIM:(hh + 1) * HEAD_DIM]
            s = lax.dot_general(qh, kg, (((1,), (1,)), ((), ())), preferred_element_type=F32)
            s = jnp.where(valid, s, NEG)
            m = jnp.max(s, axis=-1, keepdims=True)
            if has_sink:
                sk = sink_ref[hh]
                m = jnp.maximum(m, sk)
            p = jnp.exp(s - m)
            l = jnp.sum(p, axis=-1, keepdims=True)
            if has_sink:
                l = l + jnp.exp(sk - m)
            o = jnp.dot(p.astype(BF16), vg, preferred_element_type=F32) / l
            o_ref[0, :, hh * HEAD_DIM:(hh + 1) * HEAD_DIM] = o.astype(o_ref.dtype)
            if want_lse:
                lse_ref[0, :, hh * HEAD_DIM:(hh + 1) * HEAD_DIM] = jnp.broadcast_to(
                    m + jnp.log(l), (tq, HEAD_DIM))


def _banded_attention(q, k, v, *, bsz, seq_n, dil, n_col, q_blk, k_blk, q_w, k_w,
                      win, n_kv, rep, sink, want_lse, out_dtype):
    tq = ATT_TQ
    nt = seq_n // tq
    wpt = tq // win
    nwb = seq_n // win

    def qmap(b, r, i):
        return (b, i, r * n_col + q_blk)

    def kcur(b, r, i):
        return (b, i, r * n_col + k_blk)

    def kprev(b, r, i):
        return (b, jnp.maximum(i * wpt - 1, 0), r * n_col + k_blk)

    def knext(b, r, i):
        return (b, jnp.minimum((i + 1) * wpt, nwb - 1), r * n_col + k_blk)

    def omap(b, r, i):
        return (b, i, r)

    in_specs = []
    args = []
    if sink is not None:
        in_specs.append(pl.BlockSpec(memory_space=pltpu.SMEM))
        args.append(sink)
    in_specs += [pl.BlockSpec((1, tq, q_w), qmap),
                 pl.BlockSpec((1, win, k_w), kprev), pl.BlockSpec((1, tq, k_w), kcur),
                 pl.BlockSpec((1, win, k_w), knext),
                 pl.BlockSpec((1, win, k_w), kprev), pl.BlockSpec((1, tq, k_w), kcur),
                 pl.BlockSpec((1, win, k_w), knext)]
    args += [q, k, k, k, v, v, v]
    out_shape = [jax.ShapeDtypeStruct((bsz, seq_n, dil * q_w), out_dtype)]
    out_specs = [pl.BlockSpec((1, tq, q_w), omap)]
    if want_lse:
        out_shape.append(jax.ShapeDtypeStruct((bsz, seq_n, dil * q_w), F32))
        out_specs.append(pl.BlockSpec((1, tq, q_w), omap))
    kern = functools.partial(_attn_kernel, tq=tq, win=win, n_kv=n_kv, rep=rep, seq_len=seq_n,
                             has_sink=sink is not None, want_lse=want_lse)
    return pl.pallas_call(
        kern, out_shape=out_shape, grid=(bsz, dil, nt), in_specs=in_specs, out_specs=out_specs,
        compiler_params=_cparams(("arbitrary", "arbitrary", "arbitrary")), name="attn",
    )(*args)


def _oproj_kernel(*refs, n_grp):
    o_refs = refs[:n_grp]
    l_refs = refs[n_grp:2 * n_grp] if n_grp > 1 else ()
    rest = refs[2 * n_grp:] if n_grp > 1 else refs[n_grp:]
    (x_ref, wo_ref, gt_ref, g_ref, sh_ref, sc_ref, wr_ref, br_ref,
     xo_ref, h_ref, at_ref, atm_ref) = rest
    if n_grp == 1:
        o = o_refs[0][...]
    else:
        ls = [r[...] for r in l_refs]
        mx = ls[0]
        for t in ls[1:]:
            mx = jnp.maximum(mx, t)
        es = [jnp.exp(t - mx) for t in ls]
        den = es[0]
        for t in es[1:]:
            den = den + t
        o = (es[0] / den) * o_refs[0][...]
        for j in range(1, n_grp):
            o = o + (es[j] / den) * o_refs[j][...]
        o = o.astype(BF16)
    y = jnp.dot(o, wo_ref[...], preferred_element_type=F32)
    x = x_ref[...] + gt_ref[0] * y
    xo_ref[...] = x
    h = _norm_mod(x, g_ref[...], sh_ref[0], sc_ref[0])
    h_ref[...] = h
    logits = jnp.dot(h.astype(BF16), wr_ref[...], preferred_element_type=F32) + br_ref[...]
    lane = lax.broadcasted_iota(I32, logits.shape, 1)
    logits = jnp.where(lane < N_EXPERTS, logits, NEG)
    mx = jnp.max(logits, axis=-1, keepdims=True)
    ex = jnp.exp(logits - mx)
    aff = ex / jnp.sum(ex, axis=-1, keepdims=True)
    atm_ref[...] = aff
    at_ref[...] = aff.T[:N_EXPERTS, :]


def _oproj(os_, ls_, x, wo_bf16, gt, g, sh, sc, wr_pad, br_pad, seq):
    n, d = x.shape
    t = TOK_TILE
    tpb = seq // t
    n_grp = len(os_)
    ow = os_[0].shape[1]
    tok = pl.BlockSpec((t, d), lambda i: (i, 0))
    otok = pl.BlockSpec((t, ow), lambda i: (i, 0))
    vec = pl.BlockSpec((1, 1, d), lambda i: (i // tpb, 0, 0))
    row = pl.BlockSpec((1, d), lambda i: (0, 0))
    in_specs = [otok] * n_grp + ([otok] * n_grp if n_grp > 1 else []) + [
        tok, pl.BlockSpec(wo_bf16.shape, lambda i: (0, 0)), vec, row, vec, vec,
        pl.BlockSpec(wr_pad.shape, lambda i: (0, 0)), pl.BlockSpec((1, LANES), lambda i: (0, 0))]
    args = list(os_) + (list(ls_) if n_grp > 1 else []) + [
        x, wo_bf16, gt, g.reshape(1, d), sh, sc, wr_pad, br_pad]
    out_shape = [jax.ShapeDtypeStruct((n, d), F32), jax.ShapeDtypeStruct((n, d), F32),
                 jax.ShapeDtypeStruct((N_EXPERTS, n), F32), jax.ShapeDtypeStruct((n, LANES), F32)]
    out_specs = [tok, tok, pl.BlockSpec((N_EXPERTS, t), lambda i: (0, i)),
                 pl.BlockSpec((t, LANES), lambda i: (i, 0))]
    return pl.pallas_call(
        functools.partial(_oproj_kernel, n_grp=n_grp),
        out_shape=out_shape, grid=(n // t,), in_specs=in_specs, out_specs=out_specs,
        compiler_params=_cparams(("arbitrary",)), name="oproj",
    )(*args)


def _thr_kernel(a_ref, tau_ref, need_ref, *, cap):
    bits = pltpu.bitcast(a_ref[...], I32)

    def body(it, tau):
        cand = tau | jnp.left_shift(jnp.int32(1), 30 - it)
        cnt = jnp.sum(jnp.where(bits >= cand[:, :1], 1.0, 0.0), axis=1, keepdims=True)
        return jnp.where(cnt >= cap, cand, tau)

    tau = lax.fori_loop(0, 31, body, jnp.zeros((N_EXPERTS, LANES), I32))
    cgt = jnp.sum(jnp.where(bits > tau[:, :1], 1.0, 0.0), axis=1, keepdims=True)
    tau_ref[...] = tau
    need_ref[...] = jnp.broadcast_to(cap - cgt, (N_EXPERTS, LANES))


def _thresholds(aff_t, cap):
    e, n = aff_t.shape
    vm = pl.BlockSpec(memory_space=pltpu.VMEM)
    return pl.pallas_call(
        functools.partial(_thr_kernel, cap=float(cap)),
        out_shape=[jax.ShapeDtypeStruct((e, LANES), I32), jax.ShapeDtypeStruct((e, LANES), F32)],
        in_specs=[vm], out_specs=[vm, vm],
        compiler_params=pltpu.CompilerParams(vmem_limit_bytes=VMEM_LIMIT), name="thr",
    )(aff_t)


def _sel_kernel(at_ref, atm_ref, tau_ref, need_ref, idx_ref, gate_ref, cnt_ref, run_ref):
    @pl.when(pl.program_id(0) == 0)
    def _():
        run_ref[...] = jnp.zeros_like(run_ref)

    c = MOE_CHUNK
    tau = tau_ref[...]
    need = need_ref[...]
    r_io = lax.broadcasted_iota(I32, (c, c), 0)
    c_io = lax.broadcasted_iota(I32, (c, c), 1)
    upper = jnp.where(r_io < c_io, 1.0, 0.0).astype(BF16)
    ident = jnp.where(r_io == c_io, 1.0, 0.0).astype(BF16)
    lane_f = c_io.astype(F32)
    sub_f = r_io.astype(F32)
    sub16 = lax.broadcasted_iota(I32, (N_EXPERTS, LANES), 0)
    lane16 = lax.broadcasted_iota(I32, (N_EXPERTS, LANES), 1)

    def chunk(ci, cnt_acc):
        off = pl.multiple_of(ci * c, c)
        a = at_ref[:, pl.ds(off, c)]
        bits = pltpu.bitcast(a, I32)
        gt = bits > tau
        eq = bits == tau
        eqf = jnp.where(eq, 1.0, 0.0)
        eq_before = jnp.dot(eqf.astype(BF16), upper, preferred_element_type=F32) + run_ref[...]
        sel = gt | (eq & (eq_before < need))
        self_ = jnp.where(sel, 1.0, 0.0)
        cum = jnp.dot(self_.astype(BF16), upper, preferred_element_type=F32)
        key = jnp.where(sel, cum, -1.0)
        key_t = lax.dot_general(ident, key.astype(BF16), (((1,), (1,)), ((), ())),
                                preferred_element_type=F32)
        run_ref[...] = run_ref[...] + jnp.sum(eqf, axis=1, keepdims=True)
        cnt_c = jnp.sum(self_, axis=1, keepdims=True)
        cnt_acc = jnp.where(lane16 == ci, cnt_c, cnt_acc)
        atm = atm_ref[pl.ds(off, c), :]
        idx_tile = jnp.zeros((N_EXPERTS, LANES), F32)
        gate_tile = jnp.zeros((N_EXPERTS, LANES), F32)
        for e in range(N_EXPERTS):
            oh = key_t[:, e:e + 1] == lane_f
            irow = jnp.sum(jnp.where(oh, sub_f, 0.0), axis=0, keepdims=True)
            grow = jnp.sum(jnp.where(oh, atm[:, e:e + 1], 0.0), axis=0, keepdims=True)
            idx_tile = jnp.where(sub16 == e, irow, idx_tile)
            gate_tile = jnp.where(sub16 == e, grow, gate_tile)
        idx_ref[ci] = idx_tile.astype(I32)
        gate_ref[ci] = gate_tile
        return cnt_acc

    cnt = lax.fori_loop(0, N_CHUNK, chunk, jnp.zeros((N_EXPERTS, LANES), F32))
    cnt_ref[0] = cnt.astype(I32)


def _select(aff_t, aff_tm, tau, need):
    e, n = aff_t.shape
    nt = n // MOE_TILE
    full = pl.BlockSpec((e, LANES), lambda i: (0, 0))
    return pl.pallas_call(
        _sel_kernel,
        out_shape=[jax.ShapeDtypeStruct((n // MOE_CHUNK, e, LANES), I32),
                   jax.ShapeDtypeStruct((n // MOE_CHUNK, e, LANES), F32),
                   jax.ShapeDtypeStruct((nt, e, LANES), I32)],
        grid=(nt,),
        in_specs=[pl.BlockSpec((e, MOE_TILE), lambda i: (0, i)),
                  pl.BlockSpec((MOE_TILE, LANES), lambda i: (i, 0)), full, full],
        out_specs=[pl.BlockSpec((N_CHUNK, e, LANES), lambda i: (i, 0, 0)),
                   pl.BlockSpec((N_CHUNK, e, LANES), lambda i: (i, 0, 0)),
                   pl.BlockSpec((1, e, LANES), lambda i: (i, 0, 0))],
        scratch_shapes=[pltpu.VMEM((e, LANES), F32)],
        compiler_params=_cparams(("arbitrary",)), name="sel",
    )(aff_t, aff_tm, tau, need)


def _ffn_kernel(idx_ref, gate_ref, cnt_ref, h_ref, wg_ref, wu_ref, wd_ref, o_ref,
                fi_ref, fg_ref, xc_ref, ye_ref):
    t_id = pl.program_id(0)
    e_id = pl.program_id(1)

    @pl.when(e_id == 0)
    def _():
        o_ref[...] = jnp.zeros_like(o_ref)

    @pl.when((e_id == 0) & (t_id == 0))
    def _():
        xc_ref[...] = jnp.zeros_like(xc_ref)

    m = jnp.int32(0)
    for c in range(N_CHUNK):
        m_c = cnt_ref[0, 0, c]

        def cp(k, carry, c=c, base=m):
            fi_ref[base + k] = idx_ref[0, 0, c * MOE_CHUNK + k] + c * MOE_CHUNK
            fg_ref[base + k] = gate_ref[0, 0, c * MOE_CHUNK + k]
            return carry

        lax.fori_loop(0, m_c, cp, 0)
        m = m + m_c
    for j in range(SUBLANES):
        fi_ref[m + j] = 0

    d = h_ref.shape[1]
    sub = lax.broadcasted_iota(I32, (SUBLANES, d), 0)

    def gather8(kb, carry):
        tile = jnp.zeros((SUBLANES, d), F32)
        for j in range(SUBLANES):
            r = fi_ref[kb * SUBLANES + j]
            row = h_ref[pl.ds(r, 1), :]
            tile = jnp.where(sub == j, jnp.broadcast_to(row, (SUBLANES, d)), tile)
        xc_ref[pl.ds(pl.multiple_of(kb * SUBLANES, SUBLANES), SUBLANES), :] = tile
        return carry

    lax.fori_loop(0, (m + SUBLANES - 1) // SUBLANES, gather8, 0)

    def block(bi, carry):
        r0 = pl.multiple_of(bi * MOE_BLK, MOE_BLK)
        x = xc_ref[pl.ds(r0, MOE_BLK), :].astype(BF16)
        hg = jnp.dot(x, wg_ref[0], preferred_element_type=F32)
        hu = jnp.dot(x, wu_ref[0], preferred_element_type=F32)
        hid = (jax.nn.silu(hg) * hu).astype(BF16)
        ye_ref[...] = jnp.dot(hid, wd_ref[0], preferred_element_type=F32)
        rows = jnp.minimum(m - bi * MOE_BLK, MOE_BLK)

        def scatter(k, c2):
            r = fi_ref[r0 + k]
            g = fg_ref[r0 + k]
            base = pl.multiple_of((r >> 3) << 3, SUBLANES)
            row = ye_ref[pl.ds(k, 1), :] * g
            cur = o_ref[pl.ds(base, SUBLANES), :]
            o_ref[pl.ds(base, SUBLANES), :] = cur + jnp.where(
                sub == (r & 7), jnp.broadcast_to(row, (SUBLANES, d)), 0.0)
            return c2

        lax.fori_loop(0, rows, scatter, 0)
        return carry

    lax.fori_loop(0, (m + MOE_BLK - 1) // MOE_BLK, block, 0)


def _moe_ffn(h, idx_l, gate_l, cnt_l, wg, wu, wd):
    n, d = h.shape
    nt = n // MOE_TILE
    f = wg.shape[2]

    def lmap(t, e):
        return (t * N_EXPERTS + e, 0, 0)

    smem = functools.partial(pl.BlockSpec, memory_space=pltpu.SMEM)
    return pl.pallas_call(
        _ffn_kernel,
        out_shape=jax.ShapeDtypeStruct((n, d), F32),
        grid=(nt, N_EXPERTS),
        in_specs=[smem((1, 1, MOE_TILE), lmap), smem((1, 1, MOE_TILE), lmap),
                  smem((1, 1, N_CHUNK), lmap),
                  pl.BlockSpec((MOE_TILE, d), lambda t, e: (t, 0)),
                  pl.BlockSpec((1, d, f), lambda t, e: (e, 0, 0)),
                  pl.BlockSpec((1, d, f), lambda t, e: (e, 0, 0)),
                  pl.BlockSpec((1, f, d), lambda t, e: (e, 0, 0))],
        out_specs=pl.BlockSpec((MOE_TILE, d), lambda t, e: (t, 0)),
        scratch_shapes=[pltpu.SMEM((MOE_TILE + SUBLANES,), I32),
                        pltpu.SMEM((MOE_TILE + SUBLANES,), F32),
                        pltpu.VMEM((MOE_TILE, d), F32),
                        pltpu.VMEM((MOE_BLK, d), F32)],
        compiler_params=_cparams(("arbitrary", "arbitrary")), name="ffn",
    )(idx_l, gate_l, cnt_l, h, wg, wu, wd)


def _moe(h, aff_t, aff_tm, wg, wu, wd):
    n = h.shape[0]
    nt = n // MOE_TILE
    cap = max(1, EC_CAPACITY * n // N_EXPERTS)
    tau, need = _thresholds(aff_t, cap)
    idx_c, gate_c, cnt = _select(aff_t, aff_tm, tau, need)

    def relayout(a):
        a = a.reshape(nt, N_CHUNK, N_EXPERTS, MOE_CHUNK).transpose(0, 2, 1, 3)
        return a.reshape(nt * N_EXPERTS, 1, MOE_TILE)

    cnt_l = cnt[:, :, :N_CHUNK].reshape(nt * N_EXPERTS, 1, N_CHUNK)
    return _moe_ffn(h, relayout(idx_c), relayout(gate_c), cnt_l, wg, wu, wd)


def _final_kernel(x_ref, m_ref, gt_ref, g_ref, o_ref):
    x = x_ref[...] + gt_ref[0] * m_ref[...]
    ms = jnp.mean(x * x, axis=-1, keepdims=True)
    o_ref[...] = (x * lax.rsqrt(ms + EPS)) * g_ref[...]


def _final(x, moe, gt, g, seq):
    n, d = x.shape
    t = TOK_TILE
    tpb = seq // t
    tok = pl.BlockSpec((t, d), lambda i: (i, 0))
    return pl.pallas_call(
        _final_kernel, out_shape=jax.ShapeDtypeStruct((n, d), F32), grid=(n // t,),
        in_specs=[tok, tok, pl.BlockSpec((1, 1, d), lambda i: (i // tpb, 0, 0)),
                  pl.BlockSpec((1, d), lambda i: (0, 0))],
        out_specs=tok, compiler_params=_cparams(("arbitrary",)), name="final",
    )(x, moe, gt, g.reshape(1, d))


def _trunk(x3, mods, p):
    bsz, seq, d = x3.shape
    n = bsz * seq
    x = x3.reshape(n, d)
    tables = _rope_tables(seq)
    sh1, sc1, gt1, sh2, sc2, gt2 = [m[:, None, :] for m in jnp.split(mods[0], 6, axis=-1)]
    q, k, v = _qkv(x, None, p["g_mix"][0], sh1, sc1, p["w_qkv_a"], tables, seq,
                   A_HEADS * HEAD_DIM, A_KV_HEADS * HEAD_DIM)
    (o,) = _banded_attention(
        q.reshape(bsz, seq, -1), k.reshape(bsz, seq, -1), v.reshape(bsz, seq, -1),
        bsz=bsz, seq_n=seq, dil=1, n_col=1, q_blk=0, k_blk=0,
        q_w=A_HEADS * HEAD_DIM, k_w=A_KV_HEADS * HEAD_DIM, win=A_WINDOW,
        n_kv=A_KV_HEADS, rep=A_HEADS // A_KV_HEADS, sink=p["sink_a"], want_lse=False,
        out_dtype=BF16)
    x, h, aff_t, aff_tm = _oproj([o.reshape(n, -1)], None, x, p["w_o_a"], gt1, p["g_ffn"][0],
                                 sh2, sc2, p["w_router"][0], p["b_router"][0], seq)
    moe = _moe(h, aff_t, aff_tm, p["w_gate"][0], p["w_up"][0], p["w_down"][0])
    gt2_prev = gt2
    sh1, sc1, gt1, sh2, sc2, gt2 = [m[:, None, :] for m in jnp.split(mods[1], 6, axis=-1)]
    ng = len(B_GROUPS)
    x, q, k, v = _qkv(x, (moe, gt2_prev), p["g_mix"][1], sh1, sc1, p["w_qkv_b"], tables, seq,
                      ng * B_GW, ng * B_GW)
    outs, lses = [], []
    for gi, (window, dil) in enumerate(B_GROUPS):
        sn = seq // dil
        og, lg = _banded_attention(
            q.reshape(bsz, sn, -1), k.reshape(bsz, sn, -1), v.reshape(bsz, sn, -1),
            bsz=bsz, seq_n=sn, dil=dil, n_col=ng, q_blk=gi, k_blk=gi, q_w=B_GW, k_w=B_GW,
            win=(window // 2) // dil, n_kv=B_HEADS, rep=1, sink=None, want_lse=True,
            out_dtype=F32)
        outs.append(og.reshape(n, B_GW))
        lses.append(lg.reshape(n, B_GW))
    x, h, aff_t, aff_tm = _oproj(outs, lses, x, p["w_o_b"], gt1, p["g_ffn"][1],
                                 sh2, sc2, p["w_router"][1], p["b_router"][1], seq)
    moe = _moe(h, aff_t, aff_tm, p["w_gate"][1], p["w_up"][1], p["w_down"][1])
    y = _final(x, moe, gt2, p["g_final"], seq)
    return y.reshape(bsz, seq, d)


def kernel(x_prompt, x_sample, c_prompt, c_sample, w_ada, b_ada, g_mix, g_ffn, w_qkv_a, sink_a, w_o_a, w_qkv_b, w_o_b, w_router, b_router, w_gate, w_up, w_down, g_final):
    bp = c_prompt.shape[0]
    bs = c_sample.shape[0]
    rows = -(-(bp + bs) // SUBLANES) * SUBLANES
    c_all = jnp.zeros((rows, D_MODEL), F32).at[:bp].set(c_prompt).at[bp:bp + bs].set(c_sample)
    mods = _ada(c_all, w_ada, b_ada)
    depth = w_ada.shape[0]
    wr_pad = jnp.zeros((depth, D_MODEL, LANES), F32).at[:, :, :N_EXPERTS].set(w_router).astype(BF16)
    br_pad = jnp.zeros((depth, 1, LANES), F32).at[:, 0, :N_EXPERTS].set(b_router)
    p = dict(
        g_mix=g_mix, g_ffn=g_ffn, g_final=g_final, sink_a=sink_a[0],
        w_qkv_a=w_qkv_a[0].astype(BF16), w_o_a=w_o_a[0].astype(BF16),
        w_qkv_b=w_qkv_b[0].astype(BF16), w_o_b=w_o_b[0].astype(BF16),
        w_router=wr_pad, b_router=br_pad,
        w_gate=w_gate.astype(BF16), w_up=w_up.astype(BF16), w_down=w_down.astype(BF16),
    )
    y_prompt = _trunk(x_prompt, mods[:, :bp], p)
    y_sample = _trunk(x_sample, mods[:, bp:bp + bs], p)
    return (y_prompt, y_sample)
```

```python
import functools

import jax
import jax.numpy as jnp
from jax import lax
from jax.experimental import pallas as pl
from jax.experimental.pallas import tpu as pltpu

F32 = jnp.float32
BF16 = jnp.bfloat16
I32 = jnp.int32

D_MODEL = 1024
HEAD_DIM = 64
A_HEADS = 16
A_KV_HEADS = 4
A_WINDOW = 128
B_GROUPS = ((128, 1), (512, 4), (2048, 16))
B_HEADS = 8
B_GW = B_HEADS * HEAD_DIM
N_EXPERTS = 16
EXPERT_FF = 1024
EC_CAPACITY = 2
ROPE_THETA = 10000.0
EPS = 1e-6
NEG = -1e30

LANES = 128
SUBLANES = 8
VMEM_LIMIT = 56 * 1024 * 1024

TOK_TILE = 512
ATT_TQ = 256
MOE_TILE = 2048
MOE_CHUNK = 128
MOE_SB = 384
MOE_BLK = 128
N_CHUNK = MOE_TILE // MOE_CHUNK


def _cparams(sem):
    return pltpu.CompilerParams(dimension_semantics=sem, vmem_limit_bytes=VMEM_LIMIT)


def _norm_mod(x, g, sh, sc):
    ms = jnp.mean(x * x, axis=-1, keepdims=True)
    y = (x * lax.rsqrt(ms + EPS)) * g
    return y * (1.0 + sc) + sh


def _ada_kernel(c_ref, w_ref, b_ref, o_ref):
    s = jax.nn.silu(c_ref[...])
    o_ref[0] = jnp.dot(s, w_ref[0], preferred_element_type=F32,
                       precision=lax.Precision.HIGHEST) + b_ref[0]


def _ada(c_all, w_ada, b_ada):
    depth, d, n6 = w_ada.shape
    rows = c_all.shape[0]
    tn = 1536
    return pl.pallas_call(
        _ada_kernel,
        out_shape=jax.ShapeDtypeStruct((depth, rows, n6), F32),
        grid=(depth, n6 // tn),
        in_specs=[
            pl.BlockSpec((rows, d), lambda l, j: (0, 0)),
            pl.BlockSpec((1, d, tn), lambda l, j: (l, 0, j)),
            pl.BlockSpec((1, 1, tn), lambda l, j: (l, 0, j)),
        ],
        out_specs=pl.BlockSpec((1, rows, tn), lambda l, j: (l, 0, j)),
        compiler_params=_cparams(("arbitrary", "arbitrary")),
        name="ada",
    )(c_all, w_ada, b_ada.reshape(depth, 1, n6))


def _rope_tables(seq):
    half = HEAD_DIM // 2
    inv = ROPE_THETA ** (-jnp.arange(half, dtype=F32) / half)
    ang = jnp.arange(seq).astype(F32)[:, None] * inv[None, :]
    cos = jnp.cos(ang)
    sin = jnp.sin(ang)
    zero = jnp.zeros_like(sin)
    c = jnp.concatenate([cos, cos, cos, cos], axis=1)
    a = jnp.concatenate([-sin, zero, -sin, zero], axis=1)
    b = jnp.concatenate([zero, sin, zero, sin], axis=1)
    return c, a, b


def _rope_group(xg, c, a, b):
    return xg * c + pltpu.roll(xg, LANES - HEAD_DIM // 2, 1) * a + pltpu.roll(xg, HEAD_DIM // 2, 1) * b


def _qkv_kernel(*refs, pieces, has_res):
    t = TOK_TILE
    if has_res:
        x_ref, m_ref, gt_ref, g_ref, sh_ref, sc_ref, w_ref, c_ref, a_ref, b_ref, xo_ref = refs[:11]
        out_refs = refs[11:11 + len(pieces)]
        x = x_ref[...] + gt_ref[0] * m_ref[...]
        xo_ref[...] = x
    else:
        x_ref, g_ref, sh_ref, sc_ref, w_ref, c_ref, a_ref, b_ref = refs[:8]
        out_refs = refs[8:8 + len(pieces)]
        x = x_ref[...]
    s_ref = refs[-1]
    h = _norm_mod(x, g_ref[...], sh_ref[0], sc_ref[0]).astype(BF16)
    acc = jnp.dot(h, w_ref[...], preferred_element_type=F32)
    c = c_ref[...]
    a = a_ref[...]
    b = b_ref[...]
    slot = 0
    for (c0, w, rope, scale, dil), o_ref in zip(pieces, out_refs):
        ng = w // LANES
        for j in range(ng):
            xg = acc[:, c0 + j * LANES:c0 + (j + 1) * LANES]
            if rope:
                xg = _rope_group(xg, c, a, b)
            if scale != 1.0:
                xg = xg * scale
            if dil == 1:
                o_ref[:, j * LANES:(j + 1) * LANES] = xg.astype(BF16)
            else:
                s_ref[slot, j] = xg
        if dil > 1:
            rows = t // dil
            for r in range(dil):
                for j in range(ng):
                    o_ref[:, r * w + j * LANES:r * w + (j + 1) * LANES] = (
                        s_ref.at[slot, j][pl.ds(r, rows, stride=dil), :].astype(BF16))
            slot += 1


def _qkv(x, res, g, sh, sc, w_bf16, tables, seq, pieces):
    n, d = x.shape
    t = TOK_TILE
    tpb = seq // t
    c, a, b = tables
    tok = pl.BlockSpec((t, d), lambda i: (i, 0))
    vec = pl.BlockSpec((1, 1, d), lambda i: (i // tpb, 0, 0))
    tab = pl.BlockSpec((t, LANES), lambda i: (i % tpb, 0))
    in_specs = [tok]
    args = [x]
    if res is not None:
        in_specs += [tok, vec]
        args += [res[0], res[1]]
    in_specs += [pl.BlockSpec((1, d), lambda i: (0, 0)), vec, vec,
                 pl.BlockSpec(w_bf16.shape, lambda i: (0, 0)), tab, tab, tab]
    args += [g.reshape(1, d), sh, sc, w_bf16, c, a, b]
    out_shape = [jax.ShapeDtypeStruct((n // dil, dil * w), BF16) for (_, w, _, _, dil) in pieces]
    out_specs = [pl.BlockSpec((t // dil, dil * w), lambda i: (i, 0)) for (_, w, _, _, dil) in pieces]
    if res is not None:
        out_shape = [jax.ShapeDtypeStruct((n, d), F32)] + out_shape
        out_specs = [tok] + out_specs
    n_strided = max(1, sum(1 for p in pieces if p[4] > 1))
    kern = functools.partial(_qkv_kernel, pieces=pieces, has_res=res is not None)
    return pl.pallas_call(
        kern, out_shape=out_shape, grid=(n // t,), in_specs=in_specs, out_specs=out_specs,
        scratch_shapes=[pltpu.VMEM((n_strided, B_GW // LANES, t, LANES), F32)],
        compiler_params=_cparams(("arbitrary",)), name="qkv",
    )(*args)


def _attn_kernel(*refs, tq, win, n_kv, rep, seq_len, has_sink, want_lse):
    pos = 0
    if has_sink:
        sink_ref = refs[0]
        pos = 1
    q_ref, kp_ref, kc_ref, kn_ref, vp_ref, vc_ref, vn_ref = refs[pos:pos + 7]
    o_ref = refs[pos + 7]
    lse_ref = refs[pos + 8] if want_lse else None
    i = pl.program_id(2)
    tk = tq + 2 * win
    q = q_ref[0]
    k = jnp.concatenate([kp_ref[0], kc_ref[0], kn_ref[0]], axis=0)
    v = jnp.concatenate([vp_ref[0], vc_ref[0], vn_ref[0]], axis=0)
    qpos = i * tq + lax.broadcasted_iota(I32, (tq, tk), 0)
    kpos = i * tq - win + lax.broadcasted_iota(I32, (tq, tk), 1)
    valid = (jnp.abs(qpos - kpos) <= win) & (kpos >= 0) & (kpos < seq_len)
    for g in range(n_kv):
        kg = k[:, g * HEAD_DIM:(g + 1) * HEAD_DIM]
        vg = v[:, g * HEAD_DIM:(g + 1) * HEAD_DIM]
        for r in range(rep):
            hh = g * rep + r
            qh = q[:, hh * HEAD_DIM:(hh + 1) * HEAD_DIM]
            s = lax.dot_general(qh, kg, (((1,), (1,)), ((), ())), preferred_element_type=F32)
            s = jnp.where(valid, s, NEG)
            m = jnp.max(s, axis=-1, keepdims=True)
            if has_sink:
                sk = sink_ref[hh]
                m = jnp.maximum(m, sk)
            p = jnp.exp(s - m)
            l = jnp.sum(p, axis=-1, keepdims=True)
            if has_sink:
                l = l + jnp.exp(sk - m)
            o = jnp.dot(p.astype(BF16), vg, preferred_element_type=F32) / l
            o_ref[0, :, hh * HEAD_DIM:(hh + 1) * HEAD_DIM] = o.astype(o_ref.dtype)
            if want_lse:
                lse_ref[0, :, hh * HEAD_DIM:(hh + 1) * HEAD_DIM] = jnp.broadcast_to(
                    m + jnp.log(l), (tq, HEAD_DIM))


def _banded_attention(q, k, v, *, bsz, seq_n, dil, q_w, k_w,
                      win, n_kv, rep, sink, want_lse, out_dtype):
    tq = ATT_TQ
    nt = seq_n // tq
    wpt = tq // win
    nwb = seq_n // win

    def qmap(b, r, i):
        return (b, i, r)

    def kcur(b, r, i):
        return (b, i, r)

    def kprev(b, r, i):
        return (b, jnp.maximum(i * wpt - 1, 0), r)

    def knext(b, r, i):
        return (b, jnp.minimum((i + 1) * wpt, nwb - 1), r)

    def omap(b, r, i):
        return (b, i, r)

    in_specs = []
    args = []
    if sink is not None:
        in_specs.append(pl.BlockSpec(memory_space=pltpu.SMEM))
        args.append(sink)
    in_specs += [pl.BlockSpec((1, tq, q_w), qmap),
                 pl.BlockSpec((1, win, k_w), kprev), pl.BlockSpec((1, tq, k_w), kcur),
                 pl.BlockSpec((1, win, k_w), knext),
                 pl.BlockSpec((1, win, k_w), kprev), pl.BlockSpec((1, tq, k_w), kcur),
                 pl.BlockSpec((1, win, k_w), knext)]
    args += [q, k, k, k, v, v, v]
    out_shape = [jax.ShapeDtypeStruct((bsz, seq_n, dil * q_w), out_dtype)]
    out_specs = [pl.BlockSpec((1, tq, q_w), omap)]
    if want_lse:
        out_shape.append(jax.ShapeDtypeStruct((bsz, seq_n, dil * q_w), F32))
        out_specs.append(pl.BlockSpec((1, tq, q_w), omap))
    kern = functools.partial(_attn_kernel, tq=tq, win=win, n_kv=n_kv, rep=rep, seq_len=seq_n,
                             has_sink=sink is not None, want_lse=want_lse)
    return pl.pallas_call(
        kern, out_shape=out_shape, grid=(bsz, dil, nt), in_specs=in_specs, out_specs=out_specs,
        compiler_params=_cparams(("arbitrary", "arbitrary", "arbitrary")), name="attn",
    )(*args)


def _oproj_kernel(*refs, dils):
    t = TOK_TILE
    n_grp = len(dils)
    if n_grp == 1:
        o_refs, l_refs, rest = refs[:1], (), refs[1:]
    else:
        o_refs, l_refs, rest = refs[:n_grp], refs[n_grp:2 * n_grp], refs[2 * n_grp:]
    (x_ref, wo_ref, gt_ref, g_ref, sh_ref, sc_ref, wr_ref, br_ref,
     xo_ref, h_ref, at_ref, s_ref) = rest
    if n_grp == 1:
        o = o_refs[0][...]
    else:
        ng = B_GW // LANES

        def natural(ref, slot, dil):
            if dil == 1:
                return [ref[:, j * LANES:(j + 1) * LANES] for j in range(ng)]
            rows = t // dil
            for r in range(dil):
                for j in range(ng):
                    s_ref.at[slot, j][pl.ds(r, rows, stride=dil), :] = (
                        ref[:, r * B_GW + j * LANES:r * B_GW + (j + 1) * LANES])
            return [s_ref[slot, j] for j in range(ng)]

        os_ = [natural(o_refs[gi], 2 * gi, dils[gi]) for gi in range(n_grp)]
        ls_ = [natural(l_refs[gi], 2 * gi + 1, dils[gi]) for gi in range(n_grp)]
        cols = []
        for j in range(ng):
            ls = [ls_[gi][j] for gi in range(n_grp)]
            mx = ls[0]
            for v in ls[1:]:
                mx = jnp.maximum(mx, v)
            es = [jnp.exp(v - mx) for v in ls]
            den = es[0]
            for v in es[1:]:
                den = den + v
            oj = (es[0] / den) * os_[0][j]
            for gi in range(1, n_grp):
                oj = oj + (es[gi] / den) * os_[gi][j]
            cols.append(oj.astype(BF16))
        o = jnp.concatenate(cols, axis=1)
    y = jnp.dot(o, wo_ref[...], preferred_element_type=F32)
    x = x_ref[...] + gt_ref[0] * y
    xo_ref[...] = x
    h = _norm_mod(x, g_ref[...], sh_ref[0], sc_ref[0])
    hb = h.astype(BF16)
    h_ref[...] = hb
    logits = jnp.dot(hb, wr_ref[...], preferred_element_type=F32) + br_ref[...]
    lane = lax.broadcasted_iota(I32, logits.shape, 1)
    logits = jnp.where(lane < N_EXPERTS, logits, NEG)
    mx = jnp.max(logits, axis=-1, keepdims=True)
    ex = jnp.exp(logits - mx)
    aff = ex / jnp.sum(ex, axis=-1, keepdims=True)
    at_ref[...] = aff.T[:N_EXPERTS, :]


def _oproj(os_, ls_, dils, x, wo_bf16, gt, g, sh, sc, wr_pad, br_pad, seq):
    n, d = x.shape
    t = TOK_TILE
    tpb = seq // t
    n_grp = len(os_)
    tok = pl.BlockSpec((t, d), lambda i: (i, 0))
    gspecs = [pl.BlockSpec((t // dil, o.shape[1]), lambda i: (i, 0)) for o, dil in zip(os_, dils)]
    vec = pl.BlockSpec((1, 1, d), lambda i: (i // tpb, 0, 0))
    row = pl.BlockSpec((1, d), lambda i: (0, 0))
    in_specs = gspecs + (gspecs if n_grp > 1 else []) + [
        tok, pl.BlockSpec(wo_bf16.shape, lambda i: (0, 0)), vec, row, vec, vec,
        pl.BlockSpec(wr_pad.shape, lambda i: (0, 0)), pl.BlockSpec((1, LANES), lambda i: (0, 0))]
    args = list(os_) + (list(ls_) if n_grp > 1 else []) + [
        x, wo_bf16, gt, g.reshape(1, d), sh, sc, wr_pad, br_pad]
    out_shape = [jax.ShapeDtypeStruct((n, d), F32), jax.ShapeDtypeStruct((n, d), BF16),
                 jax.ShapeDtypeStruct((N_EXPERTS, n), F32)]
    out_specs = [tok, tok, pl.BlockSpec((N_EXPERTS, t), lambda i: (0, i))]
    return pl.pallas_call(
        functools.partial(_oproj_kernel, dils=tuple(dils)),
        out_shape=out_shape, grid=(n // t,), in_specs=in_specs, out_specs=out_specs,
        scratch_shapes=[pltpu.VMEM((2 * n_grp, B_GW // LANES, t, LANES), F32)],
        compiler_params=_cparams(("arbitrary",)), name="oproj",
    )(*args)


def _thr_kernel(a_ref, tau_ref, need_ref, *, cap):
    bits = pltpu.bitcast(a_ref[...], I32)

    def body(it, tau):
        cand = tau | jnp.left_shift(jnp.int32(1), 30 - it)
        cnt = jnp.sum(jnp.where(bits >= cand[:, :1], 1.0, 0.0), axis=1, keepdims=True)
        return jnp.where(cnt >= cap, cand, tau)

    tau = lax.fori_loop(0, 31, body, jnp.zeros((N_EXPERTS, LANES), I32))
    cgt = jnp.sum(jnp.where(bits > tau[:, :1], 1.0, 0.0), axis=1, keepdims=True)
    tau_ref[...] = tau
    need_ref[...] = jnp.broadcast_to(cap - cgt, (N_EXPERTS, LANES))


def _thresholds(aff_t, cap):
    e, n = aff_t.shape
    vm = pl.BlockSpec(memory_space=pltpu.VMEM)
    return pl.pallas_call(
        functools.partial(_thr_kernel, cap=float(cap)),
        out_shape=[jax.ShapeDtypeStruct((e, LANES), I32), jax.ShapeDtypeStruct((e, LANES), F32)],
        in_specs=[vm], out_specs=[vm, vm],
        compiler_params=pltpu.CompilerParams(vmem_limit_bytes=VMEM_LIMIT), name="thr",
    )(aff_t)


def _sel_kernel(at_ref, tau_ref, need_ref, kp_ref, cnt_ref, run_ref):
    @pl.when(pl.program_id(0) == 0)
    def _():
        run_ref[...] = jnp.zeros_like(run_ref)

    c = MOE_CHUNK
    tau = tau_ref[...]
    need = need_ref[...]
    r_io = lax.broadcasted_iota(I32, (c, c), 0)
    c_io = lax.broadcasted_iota(I32, (c, c), 1)
    upper = jnp.where(r_io < c_io, 1.0, 0.0).astype(BF16)

    def chunk(ci, base):
        off = pl.multiple_of(ci * c, c)
        a = at_ref[:, pl.ds(off, c)]
        bits = pltpu.bitcast(a, I32)
        gt = bits > tau
        eq = bits == tau
        eqf = jnp.where(eq, 1.0, 0.0)
        eq_before = jnp.dot(eqf.astype(BF16), upper, preferred_element_type=F32) + run_ref[...]
        sel = gt | (eq & (eq_before < need))
        self_ = jnp.where(sel, 1.0, 0.0)
        cum = jnp.dot(self_.astype(BF16), upper, preferred_element_type=F32)
        kp_ref[:, pl.ds(off, c)] = jnp.where(sel, cum + base + 1.0, 0.0)
        run_ref[...] = run_ref[...] + jnp.sum(eqf, axis=1, keepdims=True)
        return base + jnp.sum(self_, axis=1, keepdims=True)

    total = lax.fori_loop(0, N_CHUNK, chunk, jnp.zeros((N_EXPERTS, LANES), F32))
    cnt_ref[0] = total.astype(I32)


def _select(aff_t, tau, need):
    e, n = aff_t.shape
    nt = n // MOE_TILE
    full = pl.BlockSpec((e, LANES), lambda i: (0, 0))
    return pl.pallas_call(
        _sel_kernel,
        out_shape=[jax.ShapeDtypeStruct((e, n), F32),
                   jax.ShapeDtypeStruct((nt, e, LANES), I32)],
        grid=(nt,),
        in_specs=[pl.BlockSpec((e, MOE_TILE), lambda i: (0, i)), full, full],
        out_specs=[pl.BlockSpec((e, MOE_TILE), lambda i: (0, i)),
                   pl.BlockSpec((1, e, LANES), lambda i: (i, 0, 0))],
        scratch_shapes=[pltpu.VMEM((e, LANES), F32)],
        compiler_params=_cparams(("arbitrary",)), name="sel",
    )(aff_t, tau, need)


def _ffn_kernel(cnt_ref, kp_ref, at_ref, h_ref, wg_ref, wu_ref, wd_ref, o_ref, x_ref, yg_ref):
    t_id = pl.program_id(0)
    e_id = pl.program_id(1)

    @pl.when(e_id == 0)
    def _():
        o_ref[...] = jnp.zeros_like(o_ref)

    m = cnt_ref[t_id * N_EXPERTS + e_id]
    sub_sb = lax.broadcasted_iota(I32, (MOE_SB, MOE_TILE), 0).astype(F32)
    sub_blk = lax.broadcasted_iota(I32, (MOE_BLK, MOE_TILE), 0).astype(F32)

    def superblock(si, carry):
        lo = si * MOE_SB
        kp_all = kp_ref[pl.ds(e_id, 1), :]
        p = jnp.where(kp_all == sub_sb + (lo + 1).astype(F32), 1.0, 0.0).astype(BF16)
        x_ref[...] = jnp.dot(p, h_ref[...], preferred_element_type=F32).astype(BF16)
        yg_ref[...] = jnp.zeros_like(yg_ref)

        def block(bi, c2):
            r0 = pl.multiple_of(bi * MOE_BLK, MOE_BLK)
            hit = kp_all == sub_blk + (lo + r0 + 1).astype(F32)
            gate = jnp.sum(jnp.where(hit, at_ref[pl.ds(e_id, 1), :], 0.0), axis=1, keepdims=True)
            x = x_ref[pl.ds(r0, MOE_BLK), :]
            hg = jnp.dot(x, wg_ref[0], preferred_element_type=F32)
            hu = jnp.dot(x, wu_ref[0], preferred_element_type=F32)
            hid = (jax.nn.silu(hg) * hu).astype(BF16)
            ye = jnp.dot(hid, wd_ref[0], preferred_element_type=F32)
            yg_ref[pl.ds(r0, MOE_BLK), :] = (ye * gate).astype(BF16)
            return c2

        nblk = jnp.minimum((m - lo + MOE_BLK - 1) // MOE_BLK, MOE_SB // MOE_BLK)
        lax.fori_loop(0, nblk, block, 0)
        o_ref[...] += lax.dot_general(p, yg_ref[...], (((0,), (0,)), ((), ())),
                                      preferred_element_type=F32)
        return carry

    lax.fori_loop(0, (m + MOE_SB - 1) // MOE_SB, superblock, 0)


def _moe_ffn(h, kp, aff_t, cnt, wg, wu, wd):
    n, d = h.shape
    nt = n // MOE_TILE
    f = wg.shape[2]
    return pl.pallas_call(
        _ffn_kernel,
        out_shape=jax.ShapeDtypeStruct((n, d), F32),
        grid=(nt, N_EXPERTS),
        in_specs=[pl.BlockSpec(memory_space=pltpu.SMEM),
                  pl.BlockSpec((N_EXPERTS, MOE_TILE), lambda t, e: (0, t)),
                  pl.BlockSpec((N_EXPERTS, MOE_TILE), lambda t, e: (0, t)),
                  pl.BlockSpec((MOE_TILE, d), lambda t, e: (t, 0)),
                  pl.BlockSpec((1, d, f), lambda t, e: (e, 0, 0)),
                  pl.BlockSpec((1, d, f), lambda t, e: (e, 0, 0)),
                  pl.BlockSpec((1, f, d), lambda t, e: (e, 0, 0))],
        out_specs=pl.BlockSpec((MOE_TILE, d), lambda t, e: (t, 0)),
        scratch_shapes=[pltpu.VMEM((MOE_SB, d), BF16), pltpu.VMEM((MOE_SB, d), BF16)],
        compiler_params=_cparams(("arbitrary", "arbitrary")), name="ffn",
    )(cnt, kp, aff_t, h, wg, wu, wd)


def _moe(h, aff_t, wg, wu, wd):
    n = h.shape[0]
    cap = max(1, EC_CAPACITY * n // N_EXPERTS)
    tau, need = _thresholds(aff_t, cap)
    kp, cnt = _select(aff_t, tau, need)
    return _moe_ffn(h, kp, aff_t, cnt[:, :, 0].reshape(-1), wg, wu, wd)


def _final_kernel(x_ref, m_ref, gt_ref, g_ref, o_ref):
    x = x_ref[...] + gt_ref[0] * m_ref[...]
    ms = jnp.mean(x * x, axis=-1, keepdims=True)
    o_ref[...] = (x * lax.rsqrt(ms + EPS)) * g_ref[...]


def _final(x, moe, gt, g, seq):
    n, d = x.shape
    t = TOK_TILE
    tpb = seq // t
    tok = pl.BlockSpec((t, d), lambda i: (i, 0))
    return pl.pallas_call(
        _final_kernel, out_shape=jax.ShapeDtypeStruct((n, d), F32), grid=(n // t,),
        in_specs=[tok, tok, pl.BlockSpec((1, 1, d), lambda i: (i // tpb, 0, 0)),
                  pl.BlockSpec((1, d), lambda i: (0, 0))],
        out_specs=tok, compiler_params=_cparams(("arbitrary",)), name="final",
    )(x, moe, gt, g.reshape(1, d))


def _trunk(x3, mods, p):
    bsz, seq, d = x3.shape
    n = bsz * seq
    x = x3.reshape(n, d)
    tables = _rope_tables(seq)
    scale = HEAD_DIM ** -0.5
    sh1, sc1, gt1, sh2, sc2, gt2 = [m[:, None, :] for m in jnp.split(mods[0], 6, axis=-1)]
    qw, kw = A_HEADS * HEAD_DIM, A_KV_HEADS * HEAD_DIM
    pieces = ((0, qw, True, scale, 1), (qw, kw, True, 1.0, 1), (qw + kw, kw, False, 1.0, 1))
    q, k, v = _qkv(x, None, p["g_mix"][0], sh1, sc1, p["w_qkv_a"], tables, seq, pieces)
    (o,) = _banded_attention(
        q.reshape(bsz, seq, qw), k.reshape(bsz, seq, kw), v.reshape(bsz, seq, kw),
        bsz=bsz, seq_n=seq, dil=1, q_w=qw, k_w=kw, win=A_WINDOW,
        n_kv=A_KV_HEADS, rep=A_HEADS // A_KV_HEADS, sink=p["sink_a"], want_lse=False,
        out_dtype=BF16)
    x, h, aff_t = _oproj([o.reshape(n, qw)], None, (1,), x, p["w_o_a"], gt1, p["g_ffn"][0],
                         sh2, sc2, p["w_router"][0], p["b_router"][0], seq)
    moe = _moe(h, aff_t, p["w_gate"][0], p["w_up"][0], p["w_down"][0])
    gt2_prev = gt2
    sh1, sc1, gt1, sh2, sc2, gt2 = [m[:, None, :] for m in jnp.split(mods[1], 6, axis=-1)]
    ng = len(B_GROUPS)
    dils = tuple(dil for _, dil in B_GROUPS)
    pieces = tuple((part * ng * B_GW + gi * B_GW, B_GW, part < 2, scale if part == 0 else 1.0, dils[gi])
                   for gi in range(ng) for part in range(3))
    res = _qkv(x, (moe, gt2_prev), p["g_mix"][1], sh1, sc1, p["w_qkv_b"], tables, seq, pieces)
    x = res[0]
    outs, lses = [], []
    for gi, (window, dil) in enumerate(B_GROUPS):
        sn = seq // dil
        qg, kg, vg = [r.reshape(bsz, sn, dil * B_GW) for r in res[1 + 3 * gi:4 + 3 * gi]]
        og, lg = _banded_attention(
            qg, kg, vg, bsz=bsz, seq_n=sn, dil=dil, q_w=B_GW, k_w=B_GW,
            win=(window // 2) // dil, n_kv=B_HEADS, rep=1, sink=None, want_lse=True,
            out_dtype=F32)
        outs.append(og.reshape(bsz * sn, dil * B_GW))
        lses.append(lg.reshape(bsz * sn, dil * B_GW))
    x, h, aff_t = _oproj(outs, lses, dils, x, p["w_o_b"], gt1, p["g_ffn"][1],
                         sh2, sc2, p["w_router"][1], p["b_router"][1], seq)
    moe = _moe(h, aff_t, p["w_gate"][1], p["w_up"][1], p["w_down"][1])
    y = _final(x, moe, gt2, p["g_final"], seq)
    return y.reshape(bsz, seq, d)


def kernel(x_prompt, x_sample, c_prompt, c_sample, w_ada, b_ada, g_mix, g_ffn, w_qkv_a, sink_a, w_o_a, w_qkv_b, w_o_b, w_router, b_router, w_gate, w_up, w_down, g_final):
    bp = c_prompt.shape[0]
    bs = c_sample.shape[0]
    rows = -(-(bp + bs) // SUBLANES) * SUBLANES
    c_all = jnp.zeros((rows, D_MODEL), F32).at[:bp].set(c_prompt).at[bp:bp + bs].set(c_sample)
    mods = _ada(c_all, w_ada, b_ada)
    depth = w_ada.shape[0]
    wr_pad = jnp.zeros((depth, D_MODEL, LANES), F32).at[:, :, :N_EXPERTS].set(w_router).astype(BF16)
    br_pad = jnp.zeros((depth, 1, LANES), F32).at[:, 0, :N_EXPERTS].set(b_router)
    p = dict(
        g_mix=g_mix, g_ffn=g_ffn, g_final=g_final, sink_a=sink_a[0],
        w_qkv_a=w_qkv_a[0].astype(BF16), w_o_a=w_o_a[0].astype(BF16),
        w_qkv_b=w_qkv_b[0].astype(BF16), w_o_b=w_o_b[0].astype(BF16),
        w_router=wr_pad, b_router=br_pad,
        w_gate=w_gate.astype(BF16), w_up=w_up.astype(BF16), w_down=w_down.astype(BF16),
    )
    y_prompt = _trunk(x_prompt, mods[:, :bp], p)
    y_sample = _trunk(x_sample, mods[:, bp:bp + bs], p)
    return (y_prompt, y_sample)
```

```python
import functools

import jax
import jax.numpy as jnp
from jax import lax
from jax.experimental import pallas as pl
from jax.experimental.pallas import tpu as pltpu

F32 = jnp.float32
BF16 = jnp.bfloat16
I32 = jnp.int32

D_MODEL = 1024
HEAD_DIM = 64
A_HEADS = 16
A_KV_HEADS = 4
A_WINDOW = 128
B_GROUPS = ((128, 1), (512, 4), (2048, 16))
B_HEADS = 8
B_GW = B_HEADS * HEAD_DIM
N_EXPERTS = 16
EXPERT_FF = 1024
EC_CAPACITY = 2
ROPE_THETA = 10000.0
EPS = 1e-6
NEG = -1e30

LANES = 128
SUBLANES = 8
VMEM_LIMIT = 56 * 1024 * 1024

TOK_TILE = 512
ATT_TQ = 256
MOE_TILE = 2048
MOE_CHUNK = 128
MOE_SB = 384
MOE_CAP = 384
MOE_WIN = 256
MOE_SLOT = 64
N_WIN = MOE_TILE // MOE_WIN
CB_STRIDE = 32
MOE_BLK = 128
N_CHUNK = MOE_TILE // MOE_CHUNK


def _cparams(sem):
    return pltpu.CompilerParams(dimension_semantics=sem, vmem_limit_bytes=VMEM_LIMIT)


def _norm_mod(x, g, sh, sc):
    ms = jnp.mean(x * x, axis=-1, keepdims=True)
    y = (x * lax.rsqrt(ms + EPS)) * g
    return y * (1.0 + sc) + sh


def _ada_kernel(c_ref, w_ref, b_ref, o_ref):
    s = jax.nn.silu(c_ref[...])
    o_ref[0] = jnp.dot(s, w_ref[0], preferred_element_type=F32,
                       precision=lax.Precision.HIGHEST) + b_ref[0]


def _ada(c_all, w_ada, b_ada):
    depth, d, n6 = w_ada.shape
    rows = c_all.shape[0]
    tn = 1536
    return pl.pallas_call(
        _ada_kernel,
        out_shape=jax.ShapeDtypeStruct((depth, rows, n6), F32),
        grid=(depth, n6 // tn),
        in_specs=[
            pl.BlockSpec((rows, d), lambda l, j: (0, 0)),
            pl.BlockSpec((1, d, tn), lambda l, j: (l, 0, j)),
            pl.BlockSpec((1, 1, tn), lambda l, j: (l, 0, j)),
        ],
        out_specs=pl.BlockSpec((1, rows, tn), lambda l, j: (l, 0, j)),
        compiler_params=_cparams(("arbitrary", "arbitrary")),
        name="ada",
    )(c_all, w_ada, b_ada.reshape(depth, 1, n6))


def _rope_tables(seq):
    half = HEAD_DIM // 2
    inv = ROPE_THETA ** (-jnp.arange(half, dtype=F32) / half)
    ang = jnp.arange(seq).astype(F32)[:, None] * inv[None, :]
    cos = jnp.cos(ang)
    sin = jnp.sin(ang)
    zero = jnp.zeros_like(sin)
    c = jnp.concatenate([cos, cos, cos, cos], axis=1)
    a = jnp.concatenate([-sin, zero, -sin, zero], axis=1)
    b = jnp.concatenate([zero, sin, zero, sin], axis=1)
    return c, a, b


def _rope_group(xg, c, a, b):
    return xg * c + pltpu.roll(xg, LANES - HEAD_DIM // 2, 1) * a + pltpu.roll(xg, HEAD_DIM // 2, 1) * b


def _qkv_kernel(*refs, pieces, has_res):
    t = TOK_TILE
    if has_res:
        x_ref, m_ref, gt_ref, g_ref, sh_ref, sc_ref, w_ref, c_ref, a_ref, b_ref, xo_ref = refs[:11]
        out_refs = refs[11:11 + len(pieces)]
        x = x_ref[...] + gt_ref[0] * m_ref[...]
        xo_ref[...] = x
    else:
        x_ref, g_ref, sh_ref, sc_ref, w_ref, c_ref, a_ref, b_ref = refs[:8]
        out_refs = refs[8:8 + len(pieces)]
        x = x_ref[...]
    s_ref = refs[-1]
    h = _norm_mod(x, g_ref[...], sh_ref[0], sc_ref[0]).astype(BF16)
    acc = jnp.dot(h, w_ref[...], preferred_element_type=F32)
    c = c_ref[...]
    a = a_ref[...]
    b = b_ref[...]
    slot = 0
    for (c0, w, rope, scale, dil), o_ref in zip(pieces, out_refs):
        ng = w // LANES
        for j in range(ng):
            xg = acc[:, c0 + j * LANES:c0 + (j + 1) * LANES]
            if rope:
                xg = _rope_group(xg, c, a, b)
            if scale != 1.0:
                xg = xg * scale
            if dil == 1:
                o_ref[:, j * LANES:(j + 1) * LANES] = xg.astype(BF16)
            else:
                s_ref[slot, j] = xg
        if dil > 1:
            rows = t // dil
            for r in range(dil):
                for j in range(ng):
                    o_ref[:, r * w + j * LANES:r * w + (j + 1) * LANES] = (
                        s_ref.at[slot, j][pl.ds(r, rows, stride=dil), :].astype(BF16))
            slot += 1


def _qkv(x, res, g, sh, sc, w_bf16, tables, seq, pieces):
    n, d = x.shape
    t = TOK_TILE
    tpb = seq // t
    c, a, b = tables
    tok = pl.BlockSpec((t, d), lambda i: (i, 0))
    vec = pl.BlockSpec((1, 1, d), lambda i: (i // tpb, 0, 0))
    tab = pl.BlockSpec((t, LANES), lambda i: (i % tpb, 0))
    in_specs = [tok]
    args = [x]
    if res is not None:
        in_specs += [tok, vec]
        args += [res[0], res[1]]
    in_specs += [pl.BlockSpec((1, d), lambda i: (0, 0)), vec, vec,
                 pl.BlockSpec(w_bf16.shape, lambda i: (0, 0)), tab, tab, tab]
    args += [g.reshape(1, d), sh, sc, w_bf16, c, a, b]
    out_shape = [jax.ShapeDtypeStruct((n // dil, dil * w), BF16) for (_, w, _, _, dil) in pieces]
    out_specs = [pl.BlockSpec((t // dil, dil * w), lambda i: (i, 0)) for (_, w, _, _, dil) in pieces]
    if res is not None:
        out_shape = [jax.ShapeDtypeStruct((n, d), F32)] + out_shape
        out_specs = [tok] + out_specs
    n_strided = max(1, sum(1 for p in pieces if p[4] > 1))
    kern = functools.partial(_qkv_kernel, pieces=pieces, has_res=res is not None)
    return pl.pallas_call(
        kern, out_shape=out_shape, grid=(n // t,), in_specs=in_specs, out_specs=out_specs,
        scratch_shapes=[pltpu.VMEM((n_strided, B_GW // LANES, t, LANES), F32)],
        compiler_params=_cparams(("arbitrary",)), name="qkv",
    )(*args)


def _attn_kernel(*refs, tq, win, n_kv, rep, seq_len, has_sink, want_lse):
    pos = 0
    if has_sink:
        sink_ref = refs[0]
        pos = 1
    q_ref, kp_ref, kc_ref, kn_ref, vp_ref, vc_ref, vn_ref = refs[pos:pos + 7]
    o_ref = refs[pos + 7]
    lse_ref = refs[pos + 8] if want_lse else None
    i = pl.program_id(2)
    tk = tq + 2 * win
    q = q_ref[0]
    k = jnp.concatenate([kp_ref[0], kc_ref[0], kn_ref[0]], axis=0)
    v = jnp.concatenate([vp_ref[0], vc_ref[0], vn_ref[0]], axis=0)
    qpos = i * tq + lax.broadcasted_iota(I32, (tq, tk), 0)
    kpos = i * tq - win + lax.broadcasted_iota(I32, (tq, tk), 1)
    valid = (jnp.abs(qpos - kpos) <= win) & (kpos >= 0) & (kpos < seq_len)
    for g in range(n_kv):
        kg = k[:, g * HEAD_DIM:(g + 1) * HEAD_DIM]
        vg = v[:, g * HEAD_DIM:(g + 1) * HEAD_DIM]
        for r in range(rep):
            hh = g * rep + r
            qh = q[:, hh * HEAD_DIM:(hh + 1) * HEAD_DIM]
            s = lax.dot_general(qh, kg, (((1,), (1,)), ((), ())), preferred_element_type=F32)
            s = jnp.where(valid, s, NEG)
            m = jnp.max(s, axis=-1, keepdims=True)
            if has_sink:
                sk = sink_ref[hh]
                m = jnp.maximum(m, sk)
            p = jnp.exp(s - m)
            l = jnp.sum(p, axis=-1, keepdims=True)
            if has_sink:
                l = l + jnp.exp(sk - m)
            o = jnp.dot(p.astype(BF16), vg, preferred_element_type=F32) / l
            o_ref[0, :, hh * HEAD_DIM:(hh + 1) * HEAD_DIM] = o.astype(o_ref.dtype)
            if want_lse:
                lse_ref[0, :, hh * HEAD_DIM:(hh + 1) * HEAD_DIM] = jnp.broadcast_to(
                    m + jnp.log(l), (tq, HEAD_DIM))


def _banded_attention(q, k, v, *, bsz, seq_n, dil, q_w, k_w,
                      win, n_kv, rep, sink, want_lse, out_dtype):
    tq = ATT_TQ
    nt = seq_n // tq
    wpt = tq // win
    nwb = seq_n // win

    def qmap(b, r, i):
        return (b, i, r)

    def kcur(b, r, i):
        return (b, i, r)

    def kprev(b, r, i):
        return (b, jnp.maximum(i * wpt - 1, 0), r)

    def knext(b, r, i):
        return (b, jnp.minimum((i + 1) * wpt, nwb - 1), r)

    def omap(b, r, i):
        return (b, i, r)

    in_specs = []
    args = []
    if sink is not None:
        in_specs.append(pl.BlockSpec(memory_space=pltpu.SMEM))
        args.append(sink)
    in_specs += [pl.BlockSpec((1, tq, q_w), qmap),
                 pl.BlockSpec((1, win, k_w), kprev), pl.BlockSpec((1, tq, k_w), kcur),
                 pl.BlockSpec((1, win, k_w), knext),
                 pl.BlockSpec((1, win, k_w), kprev), pl.BlockSpec((1, tq, k_w), kcur),
                 pl.BlockSpec((1, win, k_w), knext)]
    args += [q, k, k, k, v, v, v]
    out_shape = [jax.ShapeDtypeStruct((bsz, seq_n, dil * q_w), out_dtype)]
    out_specs = [pl.BlockSpec((1, tq, q_w), omap)]
    if want_lse:
        out_shape.append(jax.ShapeDtypeStruct((bsz, seq_n, dil * q_w), F32))
        out_specs.append(pl.BlockSpec((1, tq, q_w), omap))
    kern = functools.partial(_attn_kernel, tq=tq, win=win, n_kv=n_kv, rep=rep, seq_len=seq_n,
                             has_sink=sink is not None, want_lse=want_lse)
    return pl.pallas_call(
        kern, out_shape=out_shape, grid=(bsz, dil, nt), in_specs=in_specs, out_specs=out_specs,
        compiler_params=_cparams(("arbitrary", "arbitrary", "arbitrary")), name="attn",
    )(*args)


def _oproj_kernel(*refs, dils):
    t = TOK_TILE
    n_grp = len(dils)
    if n_grp == 1:
        o_refs, l_refs, rest = refs[:1], (), refs[1:]
    else:
        o_refs, l_refs, rest = refs[:n_grp], refs[n_grp:2 * n_grp], refs[2 * n_grp:]
    (x_ref, wo_ref, gt_ref, g_ref, sh_ref, sc_ref, wr_ref, br_ref,
     xo_ref, h_ref, at_ref, s_ref) = rest
    if n_grp == 1:
        o = o_refs[0][...]
    else:
        ng = B_GW // LANES

        def natural(ref, slot, dil):
            if dil == 1:
                return [ref[:, j * LANES:(j + 1) * LANES] for j in range(ng)]
            rows = t // dil
            for r in range(dil):
                for j in range(ng):
                    s_ref.at[slot, j][pl.ds(r, rows, stride=dil), :] = (
                        ref[:, r * B_GW + j * LANES:r * B_GW + (j + 1) * LANES])
            return [s_ref[slot, j] for j in range(ng)]

        os_ = [natural(o_refs[gi], 2 * gi, dils[gi]) for gi in range(n_grp)]
        ls_ = [natural(l_refs[gi], 2 * gi + 1, dils[gi]) for gi in range(n_grp)]
        cols = []
        for j in range(ng):
            ls = [ls_[gi][j] for gi in range(n_grp)]
            mx = ls[0]
            for v in ls[1:]:
                mx = jnp.maximum(mx, v)
            es = [jnp.exp(v - mx) for v in ls]
            den = es[0]
            for v in es[1:]:
                den = den + v
            oj = (es[0] / den) * os_[0][j]
            for gi in range(1, n_grp):
                oj = oj + (es[gi] / den) * os_[gi][j]
            cols.append(oj.astype(BF16))
        o = jnp.concatenate(cols, axis=1)
    y = jnp.dot(o, wo_ref[...], preferred_element_type=F32)
    x = x_ref[...] + gt_ref[0] * y
    xo_ref[...] = x
    h = _norm_mod(x, g_ref[...], sh_ref[0], sc_ref[0])
    hb = h.astype(BF16)
    h_ref[...] = hb
    logits = jnp.dot(hb, wr_ref[...], preferred_element_type=F32) + br_ref[...]
    lane = lax.broadcasted_iota(I32, logits.shape, 1)
    logits = jnp.where(lane < N_EXPERTS, logits, NEG)
    mx = jnp.max(logits, axis=-1, keepdims=True)
    ex = jnp.exp(logits - mx)
    aff = ex / jnp.sum(ex, axis=-1, keepdims=True)
    at_ref[...] = aff.T[:N_EXPERTS, :]


def _oproj(os_, ls_, dils, x, wo_bf16, gt, g, sh, sc, wr_pad, br_pad, seq):
    n, d = x.shape
    t = TOK_TILE
    tpb = seq // t
    n_grp = len(os_)
    tok = pl.BlockSpec((t, d), lambda i: (i, 0))
    gspecs = [pl.BlockSpec((t // dil, o.shape[1]), lambda i: (i, 0)) for o, dil in zip(os_, dils)]
    vec = pl.BlockSpec((1, 1, d), lambda i: (i // tpb, 0, 0))
    row = pl.BlockSpec((1, d), lambda i: (0, 0))
    in_specs = gspecs + (gspecs if n_grp > 1 else []) + [
        tok, pl.BlockSpec(wo_bf16.shape, lambda i: (0, 0)), vec, row, vec, vec,
        pl.BlockSpec(wr_pad.shape, lambda i: (0, 0)), pl.BlockSpec((1, LANES), lambda i: (0, 0))]
    args = list(os_) + (list(ls_) if n_grp > 1 else []) + [
        x, wo_bf16, gt, g.reshape(1, d), sh, sc, wr_pad, br_pad]
    out_shape = [jax.ShapeDtypeStruct((n, d), F32), jax.ShapeDtypeStruct((n, d), BF16),
                 jax.ShapeDtypeStruct((N_EXPERTS, n), F32)]
    out_specs = [tok, tok, pl.BlockSpec((N_EXPERTS, t), lambda i: (0, i))]
    return pl.pallas_call(
        functools.partial(_oproj_kernel, dils=tuple(dils)),
        out_shape=out_shape, grid=(n // t,), in_specs=in_specs, out_specs=out_specs,
        scratch_shapes=[pltpu.VMEM((2 * n_grp, B_GW // LANES, t, LANES), F32)],
        compiler_params=_cparams(("arbitrary",)), name="oproj",
    )(*args)


def _thr_kernel(a_ref, tau_ref, need_ref, *, cap):
    bits = pltpu.bitcast(a_ref[...], I32)

    def body(it, tau):
        cand = tau | jnp.left_shift(jnp.int32(1), 30 - it)
        cnt = jnp.sum(jnp.where(bits >= cand[:, :1], 1.0, 0.0), axis=1, keepdims=True)
        return jnp.where(cnt >= cap, cand, tau)

    tau = lax.fori_loop(0, 31, body, jnp.zeros((N_EXPERTS, LANES), I32))
    cgt = jnp.sum(jnp.where(bits > tau[:, :1], 1.0, 0.0), axis=1, keepdims=True)
    tau_ref[...] = tau
    need_ref[...] = jnp.broadcast_to(cap - cgt, (N_EXPERTS, LANES))


def _thresholds(aff_t, cap):
    e, n = aff_t.shape
    vm = pl.BlockSpec(memory_space=pltpu.VMEM)
    return pl.pallas_call(
        functools.partial(_thr_kernel, cap=float(cap)),
        out_shape=[jax.ShapeDtypeStruct((e, LANES), I32), jax.ShapeDtypeStruct((e, LANES), F32)],
        in_specs=[vm], out_specs=[vm, vm],
        compiler_params=pltpu.CompilerParams(vmem_limit_bytes=VMEM_LIMIT), name="thr",
    )(aff_t)


def _sel_kernel(at_ref, tau_ref, need_ref, kp_ref, cnt_ref, run_ref):
    @pl.when(pl.program_id(0) == 0)
    def _():
        run_ref[...] = jnp.zeros_like(run_ref)

    c = MOE_CHUNK
    tau = tau_ref[...]
    need = need_ref[...]
    r_io = lax.broadcasted_iota(I32, (c, c), 0)
    c_io = lax.broadcasted_iota(I32, (c, c), 1)
    upper = jnp.where(r_io < c_io, 1.0, 0.0).astype(BF16)

    lane16 = lax.broadcasted_iota(I32, (N_EXPERTS, LANES), 1)

    cnt_ref[0] = jnp.zeros((N_EXPERTS, LANES), I32)

    def chunk(ci, base):
        cnt_ref[0] = jnp.where(lane16 == ci, base.astype(I32), cnt_ref[0])
        off = pl.multiple_of(ci * c, c)
        a = at_ref[:, pl.ds(off, c)]
        bits = pltpu.bitcast(a, I32)
        gt = bits > tau
        eq = bits == tau
        eqf = jnp.where(eq, 1.0, 0.0)
        eq_before = jnp.dot(eqf.astype(BF16), upper, preferred_element_type=F32) + run_ref[...]
        sel = gt | (eq & (eq_before < need))
        self_ = jnp.where(sel, 1.0, 0.0)
        cum = jnp.dot(self_.astype(BF16), upper, preferred_element_type=F32)
        kp_ref[:, pl.ds(off, c)] = jnp.where(sel, cum + base + 1.0, 0.0)
        run_ref[...] = run_ref[...] + jnp.sum(eqf, axis=1, keepdims=True)
        return base + jnp.sum(self_, axis=1, keepdims=True)

    total = lax.fori_loop(0, N_CHUNK, chunk, jnp.zeros((N_EXPERTS, LANES), F32))
    cnt_ref[0] = jnp.where(lane16 == N_CHUNK, total.astype(I32), cnt_ref[0])


def _select(aff_t, tau, need):
    e, n = aff_t.shape
    nt = n // MOE_TILE
    full = pl.BlockSpec((e, LANES), lambda i: (0, 0))
    return pl.pallas_call(
        _sel_kernel,
        out_shape=[jax.ShapeDtypeStruct((e, n), F32),
                   jax.ShapeDtypeStruct((nt, e, LANES), I32)],
        grid=(nt,),
        in_specs=[pl.BlockSpec((e, MOE_TILE), lambda i: (0, i)), full, full],
        out_specs=[pl.BlockSpec((e, MOE_TILE), lambda i: (0, i)),
                   pl.BlockSpec((1, e, LANES), lambda i: (i, 0, 0))],
        scratch_shapes=[pltpu.VMEM((e, LANES), F32)],
        compiler_params=_cparams(("arbitrary",)), name="sel",
    )(aff_t, tau, need)


def _swiglu(x, wg_ref, wu_ref, wd_ref):
    hg = jnp.dot(x, wg_ref[0], preferred_element_type=F32)
    hu = jnp.dot(x, wu_ref[0], preferred_element_type=F32)
    hid = (jax.nn.silu(hg) * hu).astype(BF16)
    return jnp.dot(hid, wd_ref[0], preferred_element_type=F32)


def _ffn_kernel(cb_ref, kp_ref, at_ref, h_ref, wg_ref, wu_ref, wd_ref, o_ref,
                xy_ref, x_ref, yg_ref):
    t_id = pl.program_id(0)
    e_id = pl.program_id(1)

    def cb(ee, lane):
        return cb_ref[(t_id * N_EXPERTS + ee) * CB_STRIDE + lane]

    m = cb(e_id, N_CHUNK)
    m_max = cb(0, N_CHUNK)
    for ee in range(1, N_EXPERTS):
        m_max = jnp.maximum(m_max, cb(ee, N_CHUNK))
    fast = m_max <= MOE_CAP
    sub_blk = lax.broadcasted_iota(I32, (MOE_BLK, MOE_TILE), 0).astype(F32)

    def gate_col(first_slot):
        hit = kp_ref[pl.ds(e_id, 1), :] == sub_blk + (first_slot + 1).astype(F32)
        return jnp.sum(jnp.where(hit, at_ref[pl.ds(e_id, 1), :], 0.0), axis=1, keepdims=True)

    cpw = MOE_WIN // MOE_CHUNK

    def win_rounds(w):
        nr = jnp.int32(0)
        for ee in range(N_EXPERTS):
            a = (cb(ee, cpw * w) >> 4) << 4
            nr = jnp.maximum(nr, (cb(ee, cpw * w + cpw) - a + MOE_SLOT - 1) // MOE_SLOT)
        return nr

    def slot_rows(w, r):
        return [jnp.minimum(((cb(ee, cpw * w) >> 4) << 4) + r * MOE_SLOT, MOE_CAP)
                for ee in range(N_EXPERTS)]

    def sel_matrix(w, rows):
        kpw = kp_ref[:, pl.ds(pl.multiple_of(w * MOE_WIN, MOE_WIN), MOE_WIN)]
        sub = lax.broadcasted_iota(I32, (MOE_SLOT, MOE_WIN), 0).astype(F32)
        parts = [jnp.where(kpw[ee:ee + 1, :] == sub + (rows[ee] + 1).astype(F32), 1.0, 0.0).astype(BF16)
                 for ee in range(N_EXPERTS)]
        return jnp.concatenate(parts, axis=0)

    def buf_off(ee, row):
        return pl.multiple_of(ee * MOE_CAP + row, 16)

    @pl.when(fast & (e_id == 0))
    def _dispatch():
        xy_ref[...] = jnp.zeros_like(xy_ref)

        def win(w, c1):
            hw = h_ref[pl.ds(pl.multiple_of(w * MOE_WIN, MOE_WIN), MOE_WIN), :]

            def rnd(r, c2):
                rows = slot_rows(w, r)
                x = jnp.dot(sel_matrix(w, rows), hw, preferred_element_type=F32)
                for ee in range(N_EXPERTS):
                    xy_ref[pl.ds(buf_off(ee, rows[ee]), MOE_SLOT), :] += (
                        x[ee * MOE_SLOT:(ee + 1) * MOE_SLOT, :].astype(BF16))
                return c2

            lax.fori_loop(0, win_rounds(w), rnd, 0)
            return c1

        lax.fori_loop(0, N_WIN, win, 0)

    @pl.when(fast)
    def _ffn_fast():
        def block(bi, c1):
            r0 = pl.multiple_of(e_id * MOE_CAP + bi * MOE_BLK, MOE_BLK)
            ye = _swiglu(xy_ref[pl.ds(r0, MOE_BLK), :], wg_ref, wu_ref, wd_ref)
            xy_ref[pl.ds(r0, MOE_BLK), :] = (ye * gate_col(bi * MOE_BLK)).astype(BF16)
            return c1

        lax.fori_loop(0, (m + MOE_BLK - 1) // MOE_BLK, block, 0)

    @pl.when(fast & (e_id == N_EXPERTS - 1))
    def _combine():
        def win(w, c1):
            w0 = pl.multiple_of(w * MOE_WIN, MOE_WIN)
            o_ref[pl.ds(w0, MOE_WIN), :] = jnp.zeros((MOE_WIN, o_ref.shape[1]), F32)

            def rnd(r, c2):
                rows = slot_rows(w, r)
                y = jnp.concatenate([xy_ref[pl.ds(buf_off(ee, rows[ee]), MOE_SLOT), :]
                                     for ee in range(N_EXPERTS)], axis=0)
                o_ref[pl.ds(w0, MOE_WIN), :] += lax.dot_general(
                    sel_matrix(w, rows), y, (((0,), (0,)), ((), ())), preferred_element_type=F32)
                return c2

            lax.fori_loop(0, win_rounds(w), rnd, 0)
            return c1

        lax.fori_loop(0, N_WIN, win, 0)

    @pl.when(jnp.logical_not(fast))
    def _general():
        @pl.when(e_id == 0)
        def _():
            o_ref[...] = jnp.zeros_like(o_ref)

        sub_sb = lax.broadcasted_iota(I32, (MOE_SB, MOE_TILE), 0).astype(F32)

        def superblock(si, carry):
            lo = si * MOE_SB
            kp_all = kp_ref[pl.ds(e_id, 1), :]
            p = jnp.where(kp_all == sub_sb + (lo + 1).astype(F32), 1.0, 0.0).astype(BF16)
            x_ref[...] = jnp.dot(p, h_ref[...], preferred_element_type=F32).astype(BF16)
            yg_ref[...] = jnp.zeros_like(yg_ref)

            def block(bi, c2):
                r0 = pl.multiple_of(bi * MOE_BLK, MOE_BLK)
                ye = _swiglu(x_ref[pl.ds(r0, MOE_BLK), :], wg_ref, wu_ref, wd_ref)
                yg_ref[pl.ds(r0, MOE_BLK), :] = (ye * gate_col(lo + r0)).astype(BF16)
                return c2

            nblk = jnp.minimum((m - lo + MOE_BLK - 1) // MOE_BLK, MOE_SB // MOE_BLK)
            lax.fori_loop(0, nblk, block, 0)
            o_ref[...] += lax.dot_general(p, yg_ref[...], (((0,), (0,)), ((), ())),
                                          preferred_element_type=F32)
            return carry

        lax.fori_loop(0, (m + MOE_SB - 1) // MOE_SB, superblock, 0)


def _moe_ffn(h, kp, aff_t, cb, wg, wu, wd):
    n, d = h.shape
    nt = n // MOE_TILE
    f = wg.shape[2]
    return pl.pallas_call(
        _ffn_kernel,
        out_shape=jax.ShapeDtypeStruct((n, d), F32),
        grid=(nt, N_EXPERTS),
        in_specs=[pl.BlockSpec(memory_space=pltpu.SMEM),
                  pl.BlockSpec((N_EXPERTS, MOE_TILE), lambda t, e: (0, t)),
                  pl.BlockSpec((N_EXPERTS, MOE_TILE), lambda t, e: (0, t)),
                  pl.BlockSpec((MOE_TILE, d), lambda t, e: (t, 0), pipeline_mode=pl.Buffered(1)),
                  pl.BlockSpec((1, d, f), lambda t, e: (e, 0, 0)),
                  pl.BlockSpec((1, d, f), lambda t, e: (e, 0, 0)),
                  pl.BlockSpec((1, f, d), lambda t, e: (e, 0, 0))],
        out_specs=pl.BlockSpec((MOE_TILE, d), lambda t, e: (t, 0), pipeline_mode=pl.Buffered(1)),
        scratch_shapes=[pltpu.VMEM((N_EXPERTS * MOE_CAP + MOE_SLOT, d), BF16),
                        pltpu.VMEM((MOE_SB, d), BF16), pltpu.VMEM((MOE_SB, d), BF16)],
        compiler_params=_cparams(("arbitrary", "arbitrary")), name="ffn",
    )(cb, kp, aff_t, h, wg, wu, wd)


def _moe(h, aff_t, wg, wu, wd):
    n = h.shape[0]
    cap = max(1, EC_CAPACITY * n // N_EXPERTS)
    tau, need = _thresholds(aff_t, cap)
    kp, cb = _select(aff_t, tau, need)
    return _moe_ffn(h, kp, aff_t, cb[:, :, :CB_STRIDE].reshape(-1), wg, wu, wd)


def _final_kernel(x_ref, m_ref, gt_ref, g_ref, o_ref):
    x = x_ref[...] + gt_ref[0] * m_ref[...]
    ms = jnp.mean(x * x, axis=-1, keepdims=True)
    o_ref[...] = (x * lax.rsqrt(ms + EPS)) * g_ref[...]


def _final(x, moe, gt, g, seq):
    n, d = x.shape
    t = TOK_TILE
    tpb = seq // t
    tok = pl.BlockSpec((t, d), lambda i: (i, 0))
    return pl.pallas_call(
        _final_kernel, out_shape=jax.ShapeDtypeStruct((n, d), F32), grid=(n // t,),
        in_specs=[tok, tok, pl.BlockSpec((1, 1, d), lambda i: (i // tpb, 0, 0)),
                  pl.BlockSpec((1, d), lambda i: (0, 0))],
        out_specs=tok, compiler_params=_cparams(("arbitrary",)), name="final",
    )(x, moe, gt, g.reshape(1, d))


def _trunk(x3, mods, p):
    bsz, seq, d = x3.shape
    n = bsz * seq
    x = x3.reshape(n, d)
    tables = _rope_tables(seq)
    scale = HEAD_DIM ** -0.5
    sh1, sc1, gt1, sh2, sc2, gt2 = [m[:, None, :] for m in jnp.split(mods[0], 6, axis=-1)]
    qw, kw = A_HEADS * HEAD_DIM, A_KV_HEADS * HEAD_DIM
    pieces = ((0, qw, True, scale, 1), (qw, kw, True, 1.0, 1), (qw + kw, kw, False, 1.0, 1))
    q, k, v = _qkv(x, None, p["g_mix"][0], sh1, sc1, p["w_qkv_a"], tables, seq, pieces)
    (o,) = _banded_attention(
        q.reshape(bsz, seq, qw), k.reshape(bsz, seq, kw), v.reshape(bsz, seq, kw),
        bsz=bsz, seq_n=seq, dil=1, q_w=qw, k_w=kw, win=A_WINDOW,
        n_kv=A_KV_HEADS, rep=A_HEADS // A_KV_HEADS, sink=p["sink_a"], want_lse=False,
        out_dtype=BF16)
    x, h, aff_t = _oproj([o.reshape(n, qw)], None, (1,), x, p["w_o_a"], gt1, p["g_ffn"][0],
                         sh2, sc2, p["w_router"][0], p["b_router"][0], seq)
    moe = _moe(h, aff_t, p["w_gate"][0], p["w_up"][0], p["w_down"][0])
    gt2_prev = gt2
    sh1, sc1, gt1, sh2, sc2, gt2 = [m[:, None, :] for m in jnp.split(mods[1], 6, axis=-1)]
    ng = len(B_GROUPS)
    dils = tuple(dil for _, dil in B_GROUPS)
    pieces = tuple((part * ng * B_GW + gi * B_GW, B_GW, part < 2, scale if part == 0 else 1.0, dils[gi])
                   for gi in range(ng) for part in range(3))
    res = _qkv(x, (moe, gt2_prev), p["g_mix"][1], sh1, sc1, p["w_qkv_b"], tables, seq, pieces)
    x = res[0]
    outs, lses = [], []
    for gi, (window, dil) in enumerate(B_GROUPS):
        sn = seq // dil
        qg, kg, vg = [r.reshape(bsz, sn, dil * B_GW) for r in res[1 + 3 * gi:4 + 3 * gi]]
        og, lg = _banded_attention(
            qg, kg, vg, bsz=bsz, seq_n=sn, dil=dil, q_w=B_GW, k_w=B_GW,
            win=(window // 2) // dil, n_kv=B_HEADS, rep=1, sink=None, want_lse=True,
            out_dtype=F32)
        outs.append(og.reshape(bsz * sn, dil * B_GW))
        lses.append(lg.reshape(bsz * sn, dil * B_GW))
    x, h, aff_t = _oproj(outs, lses, dils, x, p["w_o_b"], gt1, p["g_ffn"][1],
                         sh2, sc2, p["w_router"][1], p["b_router"][1], seq)
    moe = _moe(h, aff_t, p["w_gate"][1], p["w_up"][1], p["w_down"][1])
    y = _final(x, moe, gt2, p["g_final"], seq)
    return y.reshape(bsz, seq, d)


def kernel(x_prompt, x_sample, c_prompt, c_sample, w_ada, b_ada, g_mix, g_ffn, w_qkv_a, sink_a, w_o_a, w_qkv_b, w_o_b, w_router, b_router, w_gate, w_up, w_down, g_final):
    bp = c_prompt.shape[0]
    bs = c_sample.shape[0]
    rows = -(-(bp + bs) // SUBLANES) * SUBLANES
    c_all = jnp.zeros((rows, D_MODEL), F32).at[:bp].set(c_prompt).at[bp:bp + bs].set(c_sample)
    mods = _ada(c_all, w_ada, b_ada)
    depth = w_ada.shape[0]
    wr_pad = jnp.zeros((depth, D_MODEL, LANES), F32).at[:, :, :N_EXPERTS].set(w_router).astype(BF16)
    br_pad = jnp.zeros((depth, 1, LANES), F32).at[:, 0, :N_EXPERTS].set(b_router)
    p = dict(
        g_mix=g_mix, g_ffn=g_ffn, g_final=g_final, sink_a=sink_a[0],
        w_qkv_a=w_qkv_a[0].astype(BF16), w_o_a=w_o_a[0].astype(BF16),
        w_qkv_b=w_qkv_b[0].astype(BF16), w_o_b=w_o_b[0].astype(BF16),
        w_router=wr_pad, b_router=br_pad,
        w_gate=w_gate.astype(BF16), w_up=w_up.astype(BF16), w_down=w_down.astype(BF16),
    )
    y_prompt = _trunk(x_prompt, mods[:, :bp], p)
    y_sample = _trunk(x_sample, mods[:, bp:bp + bs], p)
    return (y_prompt, y_sample)
```

```python
import functools

import jax
import jax.numpy as jnp
from jax import lax
from jax.experimental import pallas as pl
from jax.experimental.pallas import tpu as pltpu

F32 = jnp.float32
BF16 = jnp.bfloat16
I32 = jnp.int32

D_MODEL = 1024
HEAD_DIM = 64
A_HEADS = 16
A_KV_HEADS = 4
A_WINDOW = 128
B_GROUPS = ((128, 1), (512, 4), (2048, 16))
B_HEADS = 8
B_GW = B_HEADS * HEAD_DIM
N_EXPERTS = 16
EXPERT_FF = 1024
EC_CAPACITY = 2
ROPE_THETA = 10000.0
EPS = 1e-6
NEG = -1e30

LANES = 128
SUBLANES = 8
VMEM_LIMIT = 56 * 1024 * 1024

TOK_TILE = 512
ATT_TQ = 256
ATT_QB = 64
ATT_AHEAD = 4
MOE_TILE = 2048
MOE_CHUNK = 128
MOE_SB = 384
MOE_CAP = 384
MOE_WIN = 256
MOE_SLOT = 64
N_WIN = MOE_TILE // MOE_WIN
CB_STRIDE = 32
MOE_BLK = 128
N_CHUNK = MOE_TILE // MOE_CHUNK


def _cparams(sem):
    return pltpu.CompilerParams(dimension_semantics=sem, vmem_limit_bytes=VMEM_LIMIT)


def _norm_mod(x, g, sh, sc):
    ms = jnp.mean(x * x, axis=-1, keepdims=True)
    y = (x * lax.rsqrt(ms + EPS)) * g
    return y * (1.0 + sc) + sh


def _ada_kernel(c_ref, w_ref, b_ref, o_ref):
    s = jax.nn.silu(c_ref[...])
    o_ref[0] = jnp.dot(s, w_ref[0], preferred_element_type=F32,
                       precision=lax.Precision.HIGHEST) + b_ref[0]


def _ada(c_all, w_ada, b_ada):
    depth, d, n6 = w_ada.shape
    rows = c_all.shape[0]
    tn = 1536
    return pl.pallas_call(
        _ada_kernel,
        out_shape=jax.ShapeDtypeStruct((depth, rows, n6), F32),
        grid=(depth, n6 // tn),
        in_specs=[
            pl.BlockSpec((rows, d), lambda l, j: (0, 0)),
            pl.BlockSpec((1, d, tn), lambda l, j: (l, 0, j)),
            pl.BlockSpec((1, 1, tn), lambda l, j: (l, 0, j)),
        ],
        out_specs=pl.BlockSpec((1, rows, tn), lambda l, j: (l, 0, j)),
        compiler_params=_cparams(("arbitrary", "arbitrary")),
        name="ada",
    )(c_all, w_ada, b_ada.reshape(depth, 1, n6))


def _rope_tables(seq):
    half = HEAD_DIM // 2
    inv = ROPE_THETA ** (-jnp.arange(half, dtype=F32) / half)
    ang = jnp.arange(seq).astype(F32)[:, None] * inv[None, :]
    cos = jnp.cos(ang)
    sin = jnp.sin(ang)
    zero = jnp.zeros_like(sin)
    c = jnp.concatenate([cos, cos, cos, cos], axis=1)
    a = jnp.concatenate([-sin, zero, -sin, zero], axis=1)
    b = jnp.concatenate([zero, sin, zero, sin], axis=1)
    return c, a, b


def _rope_group(xg, c, a, b):
    return xg * c + pltpu.roll(xg, LANES - HEAD_DIM // 2, 1) * a + pltpu.roll(xg, HEAD_DIM // 2, 1) * b


def _qkv_kernel(*refs, pieces, has_res):
    t = TOK_TILE
    if has_res:
        x_ref, m_ref, gt_ref, g_ref, sh_ref, sc_ref, w_ref, c_ref, a_ref, b_ref, xo_ref = refs[:11]
        out_refs = refs[11:11 + len(pieces)]
        x = x_ref[...] + gt_ref[0] * m_ref[...]
        xo_ref[...] = x
    else:
        x_ref, g_ref, sh_ref, sc_ref, w_ref, c_ref, a_ref, b_ref = refs[:8]
        out_refs = refs[8:8 + len(pieces)]
        x = x_ref[...]
    s_ref = refs[-1]
    h = _norm_mod(x, g_ref[...], sh_ref[0], sc_ref[0]).astype(BF16)
    acc = jnp.dot(h, w_ref[...], preferred_element_type=F32)
    c = c_ref[...]
    a = a_ref[...]
    b = b_ref[...]
    slot = 0
    low_half = lax.broadcasted_iota(I32, (1, LANES), 1) < HEAD_DIM
    for (c0, w, rope, scale, dil, dup), o_ref in zip(pieces, out_refs):
        ng = w // LANES
        for j in range(ng):
            xg = acc[:, c0 + j * LANES:c0 + (j + 1) * LANES]
            if rope:
                xg = _rope_group(xg, c, a, b)
            if scale != 1.0:
                xg = xg * scale
            if dup:
                sw = pltpu.roll(xg, HEAD_DIM, 1)
                o_ref[:, 2 * j * LANES:(2 * j + 1) * LANES] = jnp.where(low_half, xg, sw).astype(BF16)
                o_ref[:, (2 * j + 1) * LANES:(2 * j + 2) * LANES] = jnp.where(low_half, sw, xg).astype(BF16)
            elif dil == 1:
                o_ref[:, j * LANES:(j + 1) * LANES] = xg.astype(BF16)
            else:
                s_ref[slot, j] = xg
        if dil > 1:
            rows = t // dil
            for r in range(dil):
                for j in range(ng):
                    o_ref[:, r * w + j * LANES:r * w + (j + 1) * LANES] = (
                        s_ref.at[slot, j][pl.ds(r, rows, stride=dil), :].astype(BF16))
            slot += 1


def _qkv(x, res, g, sh, sc, w_bf16, tables, seq, pieces):
    n, d = x.shape
    t = TOK_TILE
    tpb = seq // t
    c, a, b = tables
    tok = pl.BlockSpec((t, d), lambda i: (i, 0))
    vec = pl.BlockSpec((1, 1, d), lambda i: (i // tpb, 0, 0))
    tab = pl.BlockSpec((t, LANES), lambda i: (i % tpb, 0))
    in_specs = [tok]
    args = [x]
    if res is not None:
        in_specs += [tok, vec]
        args += [res[0], res[1]]
    in_specs += [pl.BlockSpec((1, d), lambda i: (0, 0)), vec, vec,
                 pl.BlockSpec(w_bf16.shape, lambda i: (0, 0)), tab, tab, tab]
    args += [g.reshape(1, d), sh, sc, w_bf16, c, a, b]
    widths = [dil * w * (2 if dup else 1) for (_, w, _, _, dil, dup) in pieces]
    out_shape = [jax.ShapeDtypeStruct((n // p[4], wd), BF16) for p, wd in zip(pieces, widths)]
    out_specs = [pl.BlockSpec((t // p[4], wd), lambda i: (i, 0)) for p, wd in zip(pieces, widths)]
    if res is not None:
        out_shape = [jax.ShapeDtypeStruct((n, d), F32)] + out_shape
        out_specs = [tok] + out_specs
    n_strided = max(1, sum(1 for p in pieces if p[4] > 1))
    kern = functools.partial(_qkv_kernel, pieces=pieces, has_res=res is not None)
    return pl.pallas_call(
        kern, out_shape=out_shape, grid=(n // t,), in_specs=in_specs, out_specs=out_specs,
        scratch_shapes=[pltpu.VMEM((n_strided, B_GW // LANES, t, LANES), F32)],
        compiler_params=_cparams(("arbitrary",)), name="qkv",
    )(*args)


def _attn_kernel(*refs, tq, win, n_pair, pairs_per_kgroup, seq_len, has_sink, want_lse):
    pos = 0
    if has_sink:
        sink_ref = refs[0]
        pos = 1
    q_ref, kp_ref, kc_ref, kn_ref, vp_ref, vc_ref, vn_ref = refs[pos:pos + 7]
    o_ref = refs[pos + 7]
    lse_ref = refs[pos + 8] if want_lse else None
    i = pl.program_id(2)
    qb = ATT_QB
    kw = qb + 2 * win
    k = jnp.concatenate([kp_ref[0], kc_ref[0], kn_ref[0]], axis=0)
    v = jnp.concatenate([vp_ref[0], vc_ref[0], vn_ref[0]], axis=0)
    low_half = lax.broadcasted_iota(I32, (1, LANES), 1) < HEAD_DIM
    r_io = lax.broadcasted_iota(I32, (qb, kw), 0)
    c_io = lax.broadcasted_iota(I32, (qb, kw), 1)
    band = (c_io >= r_io) & (c_io - r_io <= 2 * win)
    biases = []
    for j in range(tq // qb):
        kpos = i * tq - win + j * qb + c_io
        biases.append(jnp.where(band & (kpos >= 0) & (kpos < seq_len), 0.0, NEG))
    masked = []
    for gk in range(n_pair // pairs_per_kgroup):
        kg = k[:, gk * LANES:(gk + 1) * LANES]
        vg = v[:, gk * LANES:(gk + 1) * LANES]
        zero = jnp.zeros_like(kg)
        masked.append(((jnp.where(low_half, kg, zero), jnp.where(low_half, vg, zero)),
                       (jnp.where(low_half, zero, kg), jnp.where(low_half, zero, vg))))
    units = [(p, j) for p in range(n_pair) for j in range(tq // qb)]

    def scores(u):
        p, j = u
        qp = q_ref[0, j * qb:(j + 1) * qb, p * LANES:(p + 1) * LANES]
        return [lax.dot_general(qp, kh[j * qb:j * qb + kw], (((1,), (1,)), ((), ())),
                                preferred_element_type=F32) + biases[j]
                for kh, _ in masked[p // pairs_per_kgroup]]

    def finish(u, ss):
        p, j = u
        o_pair = None
        lses = []
        for half, s in enumerate(ss):
            vh = masked[p // pairs_per_kgroup][half][1]
            m = jnp.max(s, axis=-1, keepdims=True)
            if has_sink:
                sk = sink_ref[2 * p + half]
                m = jnp.maximum(m, sk)
            e = jnp.exp(s - m)
            l = jnp.sum(e, axis=-1, keepdims=True)
            if has_sink:
                l = l + jnp.exp(sk - m)
            o_h = jnp.dot(e.astype(BF16), vh[j * qb:j * qb + kw], preferred_element_type=F32) / l
            o_pair = o_h if o_pair is None else o_pair + o_h
            lses.append(m + jnp.log(l))
        o_ref[0, j * qb:(j + 1) * qb, p * LANES:(p + 1) * LANES] = o_pair.astype(o_ref.dtype)
        if want_lse:
            lse_ref[0, j * qb:(j + 1) * qb, p * LANES:(p + 1) * LANES] = jnp.where(
                low_half, lses[0], lses[1])

    pending = [scores(u) for u in units[:ATT_AHEAD]]
    for idx, u in enumerate(units):
        if idx + ATT_AHEAD < len(units):
            pending.append(scores(units[idx + ATT_AHEAD]))
        finish(u, pending.pop(0))


def _banded_attention(q, k, v, *, bsz, seq_n, dil, q_w, k_w,
                      win, pairs_per_kgroup, sink, want_lse, out_dtype):
    tq = ATT_TQ
    nt = seq_n // tq
    wpt = tq // win
    nwb = seq_n // win

    def qmap(b, r, i):
        return (b, i, r)

    def kprev(b, r, i):
        return (b, jnp.maximum(i * wpt - 1, 0), r)

    def knext(b, r, i):
        return (b, jnp.minimum((i + 1) * wpt, nwb - 1), r)

    in_specs = []
    args = []
    if sink is not None:
        in_specs.append(pl.BlockSpec(memory_space=pltpu.SMEM))
        args.append(sink)
    in_specs += [pl.BlockSpec((1, tq, q_w), qmap),
                 pl.BlockSpec((1, win, k_w), kprev), pl.BlockSpec((1, tq, k_w), qmap),
                 pl.BlockSpec((1, win, k_w), knext),
                 pl.BlockSpec((1, win, k_w), kprev), pl.BlockSpec((1, tq, k_w), qmap),
                 pl.BlockSpec((1, win, k_w), knext)]
    args += [q, k, k, k, v, v, v]
    out_shape = [jax.ShapeDtypeStruct((bsz, seq_n, dil * q_w), out_dtype)]
    out_specs = [pl.BlockSpec((1, tq, q_w), qmap)]
    if want_lse:
        out_shape.append(jax.ShapeDtypeStruct((bsz, seq_n, dil * q_w), F32))
        out_specs.append(pl.BlockSpec((1, tq, q_w), qmap))
    kern = functools.partial(_attn_kernel, tq=tq, win=win, n_pair=q_w // LANES,
                             pairs_per_kgroup=pairs_per_kgroup, seq_len=seq_n,
                             has_sink=sink is not None, want_lse=want_lse)
    return pl.pallas_call(
        kern, out_shape=out_shape, grid=(bsz, dil, nt), in_specs=in_specs, out_specs=out_specs,
        compiler_params=_cparams(("arbitrary", "arbitrary", "arbitrary")), name="attn",
    )(*args)


def _oproj_kernel(*refs, dils):
    t = TOK_TILE
    n_grp = len(dils)
    if n_grp == 1:
        o_refs, l_refs, rest = refs[:1], (), refs[1:]
    else:
        o_refs, l_refs, rest = refs[:n_grp], refs[n_grp:2 * n_grp], refs[2 * n_grp:]
    (x_ref, wo_ref, gt_ref, g_ref, sh_ref, sc_ref, wr_ref, br_ref,
     xo_ref, h_ref, at_ref, s_ref) = rest
    if n_grp == 1:
        o = o_refs[0][...]
    else:
        ng = B_GW // LANES

        def natural(ref, slot, dil):
            if dil == 1:
                return [ref[:, j * LANES:(j + 1) * LANES] for j in range(ng)]
            rows = t // dil
            for r in range(dil):
                for j in range(ng):
                    s_ref.at[slot, j][pl.ds(r, rows, stride=dil), :] = (
                        ref[:, r * B_GW + j * LANES:r * B_GW + (j + 1) * LANES])
            return [s_ref[slot, j] for j in range(ng)]

        os_ = [natural(o_refs[gi], 2 * gi, dils[gi]) for gi in range(n_grp)]
        ls_ = [natural(l_refs[gi], 2 * gi + 1, dils[gi]) for gi in range(n_grp)]
        cols = []
        for j in range(ng):
            ls = [ls_[gi][j] for gi in range(n_grp)]
            mx = ls[0]
            for v in ls[1:]:
                mx = jnp.maximum(mx, v)
            es = [jnp.exp(v - mx) for v in ls]
            den = es[0]
            for v in es[1:]:
                den = den + v
            oj = (es[0] / den) * os_[0][j]
            for gi in range(1, n_grp):
                oj = oj + (es[gi] / den) * os_[gi][j]
            cols.append(oj.astype(BF16))
        o = jnp.concatenate(cols, axis=1)
    y = jnp.dot(o, wo_ref[...], preferred_element_type=F32)
    x = x_ref[...] + gt_ref[0] * y
    xo_ref[...] = x
    h = _norm_mod(x, g_ref[...], sh_ref[0], sc_ref[0])
    hb = h.astype(BF16)
    h_ref[...] = hb
    logits = jnp.dot(hb, wr_ref[...], preferred_element_type=F32) + br_ref[...]
    lane = lax.broadcasted_iota(I32, logits.shape, 1)
    logits = jnp.where(lane < N_EXPERTS, logits, NEG)
    mx = jnp.max(logits, axis=-1, keepdims=True)
    ex = jnp.exp(logits - mx)
    aff = ex / jnp.sum(ex, axis=-1, keepdims=True)
    at_ref[...] = aff.T[:N_EXPERTS, :]


def _oproj(os_, ls_, dils, x, wo_bf16, gt, g, sh, sc, wr_pad, br_pad, seq):
    n, d = x.shape
    t = TOK_TILE
    tpb = seq // t
    n_grp = len(os_)
    tok = pl.BlockSpec((t, d), lambda i: (i, 0))
    gspecs = [pl.BlockSpec((t // dil, o.shape[1]), lambda i: (i, 0)) for o, dil in zip(os_, dils)]
    vec = pl.BlockSpec((1, 1, d), lambda i: (i // tpb, 0, 0))
    row = pl.BlockSpec((1, d), lambda i: (0, 0))
    in_specs = gspecs + (gspecs if n_grp > 1 else []) + [
        tok, pl.BlockSpec(wo_bf16.shape, lambda i: (0, 0)), vec, row, vec, vec,
        pl.BlockSpec(wr_pad.shape, lambda i: (0, 0)), pl.BlockSpec((1, LANES), lambda i: (0, 0))]
    args = list(os_) + (list(ls_) if n_grp > 1 else []) + [
        x, wo_bf16, gt, g.reshape(1, d), sh, sc, wr_pad, br_pad]
    out_shape = [jax.ShapeDtypeStruct((n, d), F32), jax.ShapeDtypeStruct((n, d), BF16),
                 jax.ShapeDtypeStruct((N_EXPERTS, n), F32)]
    out_specs = [tok, tok, pl.BlockSpec((N_EXPERTS, t), lambda i: (0, i))]
    return pl.pallas_call(
        functools.partial(_oproj_kernel, dils=tuple(dils)),
        out_shape=out_shape, grid=(n // t,), in_specs=in_specs, out_specs=out_specs,
        scratch_shapes=[pltpu.VMEM((2 * n_grp, B_GW // LANES, t, LANES), F32)],
        compiler_params=_cparams(("arbitrary",)), name="oproj",
    )(*args)


def _thr_kernel(a_ref, tau_ref, need_ref, *, cap):
    bits = pltpu.bitcast(a_ref[...], I32)

    def body(it, tau):
        cand = tau | jnp.left_shift(jnp.int32(1), 30 - it)
        cnt = jnp.sum(jnp.where(bits >= cand[:, :1], 1.0, 0.0), axis=1, keepdims=True)
        return jnp.where(cnt >= cap, cand, tau)

    tau = lax.fori_loop(0, 31, body, jnp.zeros((N_EXPERTS, LANES), I32))
    cgt = jnp.sum(jnp.where(bits > tau[:, :1], 1.0, 0.0), axis=1, keepdims=True)
    tau_ref[...] = tau
    need_ref[...] = jnp.broadcast_to(cap - cgt, (N_EXPERTS, LANES))


def _thresholds(aff_t, cap):
    e, n = aff_t.shape
    vm = pl.BlockSpec(memory_space=pltpu.VMEM)
    return pl.pallas_call(
        functools.partial(_thr_kernel, cap=float(cap)),
        out_shape=[jax.ShapeDtypeStruct((e, LANES), I32), jax.ShapeDtypeStruct((e, LANES), F32)],
        in_specs=[vm], out_specs=[vm, vm],
        compiler_params=pltpu.CompilerParams(vmem_limit_bytes=VMEM_LIMIT), name="thr",
    )(aff_t)


def _sel_kernel(at_ref, tau_ref, need_ref, kp_ref, cnt_ref, run_ref):
    @pl.when(pl.program_id(0) == 0)
    def _():
        run_ref[...] = jnp.zeros_like(run_ref)

    c = MOE_CHUNK
    tau = tau_ref[...]
    need = need_ref[...]
    r_io = lax.broadcasted_iota(I32, (c, c), 0)
    c_io = lax.broadcasted_iota(I32, (c, c), 1)
    upper = jnp.where(r_io < c_io, 1.0, 0.0).astype(BF16)

    lane16 = lax.broadcasted_iota(I32, (N_EXPERTS, LANES), 1)

    cnt_ref[0] = jnp.zeros((N_EXPERTS, LANES), I32)

    def chunk(ci, base):
        cnt_ref[0] = jnp.where(lane16 == ci, base.astype(I32), cnt_ref[0])
        off = pl.multiple_of(ci * c, c)
        a = at_ref[:, pl.ds(off, c)]
        bits = pltpu.bitcast(a, I32)
        gt = bits > tau
        eq = bits == tau
        eqf = jnp.where(eq, 1.0, 0.0)
        eq_before = jnp.dot(eqf.astype(BF16), upper, preferred_element_type=F32) + run_ref[...]
        sel = gt | (eq & (eq_before < need))
        self_ = jnp.where(sel, 1.0, 0.0)
        cum = jnp.dot(self_.astype(BF16), upper, preferred_element_type=F32)
        kp_ref[:, pl.ds(off, c)] = jnp.where(sel, cum + base + 1.0, 0.0)
        run_ref[...] = run_ref[...] + jnp.sum(eqf, axis=1, keepdims=True)
        return base + jnp.sum(self_, axis=1, keepdims=True)

    total = lax.fori_loop(0, N_CHUNK, chunk, jnp.zeros((N_EXPERTS, LANES), F32))
    cnt_ref[0] = jnp.where(lane16 == N_CHUNK, total.astype(I32), cnt_ref[0])


def _select(aff_t, tau, need):
    e, n = aff_t.shape
    nt = n // MOE_TILE
    full = pl.BlockSpec((e, LANES), lambda i: (0, 0))
    return pl.pallas_call(
        _sel_kernel,
        out_shape=[jax.ShapeDtypeStruct((e, n), F32),
                   jax.ShapeDtypeStruct((nt, e, LANES), I32)],
        grid=(nt,),
        in_specs=[pl.BlockSpec((e, MOE_TILE), lambda i: (0, i)), full, full],
        out_specs=[pl.BlockSpec((e, MOE_TILE), lambda i: (0, i)),
                   pl.BlockSpec((1, e, LANES), lambda i: (i, 0, 0))],
        scratch_shapes=[pltpu.VMEM((e, LANES), F32)],
        compiler_params=_cparams(("arbitrary",)), name="sel",
    )(aff_t, tau, need)


def _swiglu(x, wg_ref, wu_ref, wd_ref):
    hg = jnp.dot(x, wg_ref[0], preferred_element_type=F32)
    hu = jnp.dot(x, wu_ref[0], preferred_element_type=F32)
    hid = (jax.nn.silu(hg) * hu).astype(BF16)
    return jnp.dot(hid, wd_ref[0], preferred_element_type=F32)


def _ffn_kernel(cb_ref, kp_ref, at_ref, h_ref, wg_ref, wu_ref, wd_ref, o_ref,
                xy_ref, x_ref, yg_ref):
    t_id = pl.program_id(0)
    e_id = pl.program_id(1)

    def cb(ee, lane):
        return cb_ref[(t_id * N_EXPERTS + ee) * CB_STRIDE + lane]

    m = cb(e_id, N_CHUNK)
    m_max = cb(0, N_CHUNK)
    for ee in range(1, N_EXPERTS):
        m_max = jnp.maximum(m_max, cb(ee, N_CHUNK))
    fast = m_max <= MOE_CAP
    sub_blk = lax.broadcasted_iota(I32, (MOE_BLK, MOE_TILE), 0).astype(F32)

    def gate_col(first_slot):
        hit = kp_ref[pl.ds(e_id, 1), :] == sub_blk + (first_slot + 1).astype(F32)
        return jnp.sum(jnp.where(hit, at_ref[pl.ds(e_id, 1), :], 0.0), axis=1, keepdims=True)

    cpw = MOE_WIN // MOE_CHUNK

    def win_rounds(w):
        nr = jnp.int32(0)
        for ee in range(N_EXPERTS):
            a = (cb(ee, cpw * w) >> 4) << 4
            nr = jnp.maximum(nr, (cb(ee, cpw * w + cpw) - a + MOE_SLOT - 1) // MOE_SLOT)
        return nr

    def slot_rows(w, r):
        return [jnp.minimum(((cb(ee, cpw * w) >> 4) << 4) + r * MOE_SLOT, MOE_CAP)
                for ee in range(N_EXPERTS)]

    def sel_matrix(w, rows):
        kpw = kp_ref[:, pl.ds(pl.multiple_of(w * MOE_WIN, MOE_WIN), MOE_WIN)]
        sub = lax.broadcasted_iota(I32, (MOE_SLOT, MOE_WIN), 0).astype(F32)
        parts = [jnp.where(kpw[ee:ee + 1, :] == sub + (rows[ee] + 1).astype(F32), 1.0, 0.0).astype(BF16)
                 for ee in range(N_EXPERTS)]
        return jnp.concatenate(parts, axis=0)

    def buf_off(ee, row):
        return pl.multiple_of(ee * MOE_CAP + row, 16)

    @pl.when(fast & (e_id == 0))
    def _dispatch():
        xy_ref[...] = jnp.zeros_like(xy_ref)

        def win(w, c1):
            hw = h_ref[pl.ds(pl.multiple_of(w * MOE_WIN, MOE_WIN), MOE_WIN), :]

            def rnd(r, c2):
                rows = slot_rows(w, r)
                x = jnp.dot(sel_matrix(w, rows), hw, preferred_element_type=F32)
                for ee in range(N_EXPERTS):
                    xy_ref[pl.ds(buf_off(ee, rows[ee]), MOE_SLOT), :] += (
                        x[ee * MOE_SLOT:(ee + 1) * MOE_SLOT, :].astype(BF16))
                return c2

            lax.fori_loop(0, win_rounds(w), rnd, 0)
            return c1

        lax.fori_loop(0, N_WIN, win, 0)

    @pl.when(fast)
    def _ffn_fast():
        def block(bi, c1):
            r0 = pl.multiple_of(e_id * MOE_CAP + bi * MOE_BLK, MOE_BLK)
            ye = _swiglu(xy_ref[pl.ds(r0, MOE_BLK), :], wg_ref, wu_ref, wd_ref)
            xy_ref[pl.ds(r0, MOE_BLK), :] = (ye * gate_col(bi * MOE_BLK)).astype(BF16)
            return c1

        lax.fori_loop(0, (m + MOE_BLK - 1) // MOE_BLK, block, 0)

    @pl.when(fast & (e_id == N_EXPERTS - 1))
    def _combine():
        def win(w, c1):
            w0 = pl.multiple_of(w * MOE_WIN, MOE_WIN)
            o_ref[pl.ds(w0, MOE_WIN), :] = jnp.zeros((MOE_WIN, o_ref.shape[1]), F32)

            def rnd(r, c2):
                rows = slot_rows(w, r)
                y = jnp.concatenate([xy_ref[pl.ds(buf_off(ee, rows[ee]), MOE_SLOT), :]
                                     for ee in range(N_EXPERTS)], axis=0)
                o_ref[pl.ds(w0, MOE_WIN), :] += lax.dot_general(
                    sel_matrix(w, rows), y, (((0,), (0,)), ((), ())), preferred_element_type=F32)
                return c2

            lax.fori_loop(0, win_rounds(w), rnd, 0)
            return c1

        lax.fori_loop(0, N_WIN, win, 0)

    @pl.when(jnp.logical_not(fast))
    def _general():
        @pl.when(e_id == 0)
        def _():
            o_ref[...] = jnp.zeros_like(o_ref)

        sub_sb = lax.broadcasted_iota(I32, (MOE_SB, MOE_TILE), 0).astype(F32)

        def superblock(si, carry):
            lo = si * MOE_SB
            kp_all = kp_ref[pl.ds(e_id, 1), :]
            p = jnp.where(kp_all == sub_sb + (lo + 1).astype(F32), 1.0, 0.0).astype(BF16)
            x_ref[...] = jnp.dot(p, h_ref[...], preferred_element_type=F32).astype(BF16)
            yg_ref[...] = jnp.zeros_like(yg_ref)

            def block(bi, c2):
                r0 = pl.multiple_of(bi * MOE_BLK, MOE_BLK)
                ye = _swiglu(x_ref[pl.ds(r0, MOE_BLK), :], wg_ref, wu_ref, wd_ref)
                yg_ref[pl.ds(r0, MOE_BLK), :] = (ye * gate_col(lo + r0)).astype(BF16)
                return c2

            nblk = jnp.minimum((m - lo + MOE_BLK - 1) // MOE_BLK, MOE_SB // MOE_BLK)
            lax.fori_loop(0, nblk, block, 0)
            o_ref[...] += lax.dot_general(p, yg_ref[...], (((0,), (0,)), ((), ())),
                                          preferred_element_type=F32)
            return carry

        lax.fori_loop(0, (m + MOE_SB - 1) // MOE_SB, superblock, 0)


def _moe_ffn(h, kp, aff_t, cb, wg, wu, wd):
    n, d = h.shape
    nt = n // MOE_TILE
    f = wg.shape[2]
    return pl.pallas_call(
        _ffn_kernel,
        out_shape=jax.ShapeDtypeStruct((n, d), F32),
        grid=(nt, N_EXPERTS),
        in_specs=[pl.BlockSpec(memory_space=pltpu.SMEM),
                  pl.BlockSpec((N_EXPERTS, MOE_TILE), lambda t, e: (0, t)),
                  pl.BlockSpec((N_EXPERTS, MOE_TILE), lambda t, e: (0, t)),
                  pl.BlockSpec((MOE_TILE, d), lambda t, e: (t, 0), pipeline_mode=pl.Buffered(1)),
                  pl.BlockSpec((1, d, f), lambda t, e: (e, 0, 0)),
                  pl.BlockSpec((1, d, f), lambda t, e: (e, 0, 0)),
                  pl.BlockSpec((1, f, d), lambda t, e: (e, 0, 0))],
        out_specs=pl.BlockSpec((MOE_TILE, d), lambda t, e: (t, 0), pipeline_mode=pl.Buffered(1)),
        scratch_shapes=[pltpu.VMEM((N_EXPERTS * MOE_CAP + MOE_SLOT, d), BF16),
                        pltpu.VMEM((MOE_SB, d), BF16), pltpu.VMEM((MOE_SB, d), BF16)],
        compiler_params=_cparams(("arbitrary", "arbitrary")), name="ffn",
    )(cb, kp, aff_t, h, wg, wu, wd)


def _moe(h, aff_t, wg, wu, wd):
    n = h.shape[0]
    cap = max(1, EC_CAPACITY * n // N_EXPERTS)
    tau, need = _thresholds(aff_t, cap)
    kp, cb = _select(aff_t, tau, need)
    return _moe_ffn(h, kp, aff_t, cb[:, :, :CB_STRIDE].reshape(-1), wg, wu, wd)


def _final_kernel(x_ref, m_ref, gt_ref, g_ref, o_ref):
    x = x_ref[...] + gt_ref[0] * m_ref[...]
    ms = jnp.mean(x * x, axis=-1, keepdims=True)
    o_ref[...] = (x * lax.rsqrt(ms + EPS)) * g_ref[...]


def _final(x, moe, gt, g, seq):
    n, d = x.shape
    t = TOK_TILE
    tpb = seq // t
    tok = pl.BlockSpec((t, d), lambda i: (i, 0))
    return pl.pallas_call(
        _final_kernel, out_shape=jax.ShapeDtypeStruct((n, d), F32), grid=(n // t,),
        in_specs=[tok, tok, pl.BlockSpec((1, 1, d), lambda i: (i // tpb, 0, 0)),
                  pl.BlockSpec((1, d), lambda i: (0, 0))],
        out_specs=tok, compiler_params=_cparams(("arbitrary",)), name="final",
    )(x, moe, gt, g.reshape(1, d))


def _trunk(x3, mods, p):
    bsz, seq, d = x3.shape
    n = bsz * seq
    x = x3.reshape(n, d)
    tables = _rope_tables(seq)
    scale = HEAD_DIM ** -0.5
    sh1, sc1, gt1, sh2, sc2, gt2 = [m[:, None, :] for m in jnp.split(mods[0], 6, axis=-1)]
    qw, kw = A_HEADS * HEAD_DIM, A_KV_HEADS * HEAD_DIM
    pieces = ((0, qw, True, scale, 1, False), (qw, kw, True, 1.0, 1, True),
              (qw + kw, kw, False, 1.0, 1, True))
    q, k, v = _qkv(x, None, p["g_mix"][0], sh1, sc1, p["w_qkv_a"], tables, seq, pieces)
    (o,) = _banded_attention(
        q.reshape(bsz, seq, qw), k.reshape(bsz, seq, 2 * kw), v.reshape(bsz, seq, 2 * kw),
        bsz=bsz, seq_n=seq, dil=1, q_w=qw, k_w=2 * kw, win=A_WINDOW,
        pairs_per_kgroup=A_HEADS // A_KV_HEADS // 2, sink=p["sink_a"], want_lse=False,
        out_dtype=BF16)
    x, h, aff_t = _oproj([o.reshape(n, qw)], None, (1,), x, p["w_o_a"], gt1, p["g_ffn"][0],
                         sh2, sc2, p["w_router"][0], p["b_router"][0], seq)
    moe = _moe(h, aff_t, p["w_gate"][0], p["w_up"][0], p["w_down"][0])
    gt2_prev = gt2
    sh1, sc1, gt1, sh2, sc2, gt2 = [m[:, None, :] for m in jnp.split(mods[1], 6, axis=-1)]
    ng = len(B_GROUPS)
    dils = tuple(dil for _, dil in B_GROUPS)
    pieces = tuple((part * ng * B_GW + gi * B_GW, B_GW, part < 2, scale if part == 0 else 1.0, dils[gi], False)
                   for gi in range(ng) for part in range(3))
    res = _qkv(x, (moe, gt2_prev), p["g_mix"][1], sh1, sc1, p["w_qkv_b"], tables, seq, pieces)
    x = res[0]
    outs, lses = [], []
    for gi, (window, dil) in enumerate(B_GROUPS):
        sn = seq // dil
        qg, kg, vg = [r.reshape(bsz, sn, dil * B_GW) for r in res[1 + 3 * gi:4 + 3 * gi]]
        og, lg = _banded_attention(
            qg, kg, vg, bsz=bsz, seq_n=sn, dil=dil, q_w=B_GW, k_w=B_GW,
            win=(window // 2) // dil, pairs_per_kgroup=1, sink=None, want_lse=True,
            out_dtype=F32)
        outs.append(og.reshape(bsz * sn, dil * B_GW))
        lses.append(lg.reshape(bsz * sn, dil * B_GW))
    x, h, aff_t = _oproj(outs, lses, dils, x, p["w_o_b"], gt1, p["g_ffn"][1],
                         sh2, sc2, p["w_router"][1], p["b_router"][1], seq)
    moe = _moe(h, aff_t, p["w_gate"][1], p["w_up"][1], p["w_down"][1])
    y = _final(x, moe, gt2, p["g_final"], seq)
    return y.reshape(bsz, seq, d)


def kernel(x_prompt, x_sample, c_prompt, c_sample, w_ada, b_ada, g_mix, g_ffn, w_qkv_a, sink_a, w_o_a, w_qkv_b, w_o_b, w_router, b_router, w_gate, w_up, w_down, g_final):
    bp = c_prompt.shape[0]
    bs = c_sample.shape[0]
    rows = -(-(bp + bs) // SUBLANES) * SUBLANES
    c_all = jnp.zeros((rows, D_MODEL), F32).at[:bp].set(c_prompt).at[bp:bp + bs].set(c_sample)
    mods = _ada(c_all, w_ada, b_ada)
    depth = w_ada.shape[0]
    wr_pad = jnp.zeros((depth, D_MODEL, LANES), F32).at[:, :, :N_EXPERTS].set(w_router).astype(BF16)
    br_pad = jnp.zeros((depth, 1, LANES), F32).at[:, 0, :N_EXPERTS].set(b_router)
    p = dict(
        g_mix=g_mix, g_ffn=g_ffn, g_final=g_final, sink_a=sink_a[0],
        w_qkv_a=w_qkv_a[0].astype(BF16), w_o_a=w_o_a[0].astype(BF16),
        w_qkv_b=w_qkv_b[0].astype(BF16), w_o_b=w_o_b[0].astype(BF16),
        w_router=wr_pad, b_router=br_pad,
        w_gate=w_gate.astype(BF16), w_up=w_up.astype(BF16), w_down=w_down.astype(BF16),
    )
    y_prompt = _trunk(x_prompt, mods[:, :bp], p)
    y_sample = _trunk(x_sample, mods[:, bp:bp + bs], p)
    return (y_prompt, y_sample)
```

```python
import functools

import jax
import jax.numpy as jnp
from jax import lax
from jax.experimental import pallas as pl
from jax.experimental.pallas import tpu as pltpu

F32 = jnp.float32
BF16 = jnp.bfloat16
I32 = jnp.int32

D_MODEL = 1024
HEAD_DIM = 64
A_HEADS = 16
A_KV_HEADS = 4
A_WINDOW = 128
B_GROUPS = ((128, 1), (512, 4), (2048, 16))
B_HEADS = 8
B_GW = B_HEADS * HEAD_DIM
N_EXPERTS = 16
EXPERT_FF = 1024
EC_CAPACITY = 2
ROPE_THETA = 10000.0
EPS = 1e-6
NEG = -1e30

LANES = 128
SUBLANES = 8
VMEM_LIMIT = 56 * 1024 * 1024

TOK_TILE = 512
ATT_TQ = 256
ATT_QB = 128
ATT_AHEAD = 3
MOE_TILE = 2048
MOE_CHUNK = 128
MOE_SB = 384
MOE_CAP = 384
MOE_WIN = 256
MOE_SLOT = 64
N_WIN = MOE_TILE // MOE_WIN
CB_STRIDE = 32
MOE_BLK = 128
MOE_HEAD = 304
N_CHUNK = MOE_TILE // MOE_CHUNK


def _cparams(sem):
    return pltpu.CompilerParams(dimension_semantics=sem, vmem_limit_bytes=VMEM_LIMIT)


def _norm_mod(x, g, sh, sc):
    ms = jnp.mean(x * x, axis=-1, keepdims=True)
    y = (x * lax.rsqrt(ms + EPS)) * g
    return y * (1.0 + sc) + sh


def _ada_kernel(c_ref, w_ref, b_ref, o_ref):
    s = jax.nn.silu(c_ref[...])
    o_ref[0] = jnp.dot(s, w_ref[0], preferred_element_type=F32,
                       precision=lax.Precision.HIGHEST) + b_ref[0]


def _ada(c_all, w_ada, b_ada):
    depth, d, n6 = w_ada.shape
    rows = c_all.shape[0]
    tn = 1536
    return pl.pallas_call(
        _ada_kernel,
        out_shape=jax.ShapeDtypeStruct((depth, rows, n6), F32),
        grid=(depth, n6 // tn),
        in_specs=[
            pl.BlockSpec((rows, d), lambda l, j: (0, 0)),
            pl.BlockSpec((1, d, tn), lambda l, j: (l, 0, j)),
            pl.BlockSpec((1, 1, tn), lambda l, j: (l, 0, j)),
        ],
        out_specs=pl.BlockSpec((1, rows, tn), lambda l, j: (l, 0, j)),
        compiler_params=_cparams(("arbitrary", "arbitrary")),
        name="ada",
    )(c_all, w_ada, b_ada.reshape(depth, 1, n6))


def _rope_tables(seq):
    half = HEAD_DIM // 2
    inv = ROPE_THETA ** (-jnp.arange(half, dtype=F32) / half)
    ang = jnp.arange(seq).astype(F32)[:, None] * inv[None, :]
    cos = jnp.cos(ang)
    sin = jnp.sin(ang)
    zero = jnp.zeros_like(sin)
    c = jnp.concatenate([cos, cos, cos, cos], axis=1)
    a = jnp.concatenate([-sin, zero, -sin, zero], axis=1)
    b = jnp.concatenate([zero, sin, zero, sin], axis=1)
    return c, a, b


def _rope_group(xg, c, a, b):
    return xg * c + pltpu.roll(xg, LANES - HEAD_DIM // 2, 1) * a + pltpu.roll(xg, HEAD_DIM // 2, 1) * b


def _qkv_kernel(*refs, pieces, has_res):
    t = TOK_TILE
    if has_res:
        x_ref, m_ref, gt_ref, g_ref, sh_ref, sc_ref, w_ref, c_ref, a_ref, b_ref, xo_ref = refs[:11]
        out_refs = refs[11:11 + len(pieces)]
        x = x_ref[...] + gt_ref[0] * m_ref[...]
        xo_ref[...] = x
    else:
        x_ref, g_ref, sh_ref, sc_ref, w_ref, c_ref, a_ref, b_ref = refs[:8]
        out_refs = refs[8:8 + len(pieces)]
        x = x_ref[...]
    s_ref = refs[-1]
    h = _norm_mod(x, g_ref[...], sh_ref[0], sc_ref[0]).astype(BF16)
    acc = jnp.dot(h, w_ref[...], preferred_element_type=F32)
    c = c_ref[...]
    a = a_ref[...]
    b = b_ref[...]
    slot = 0
    low_half = lax.broadcasted_iota(I32, (1, LANES), 1) < HEAD_DIM
    for (c0, w, rope, scale, dil, dup), o_ref in zip(pieces, out_refs):
        ng = w // LANES
        for j in range(ng):
            xg = acc[:, c0 + j * LANES:c0 + (j + 1) * LANES]
            if rope:
                xg = _rope_group(xg, c, a, b)
            if scale != 1.0:
                xg = xg * scale
            if dup:
                sw = pltpu.roll(xg, HEAD_DIM, 1)
                o_ref[:, 2 * j * LANES:(2 * j + 1) * LANES] = jnp.where(low_half, xg, sw).astype(BF16)
                o_ref[:, (2 * j + 1) * LANES:(2 * j + 2) * LANES] = jnp.where(low_half, sw, xg).astype(BF16)
            elif dil == 1:
                o_ref[:, j * LANES:(j + 1) * LANES] = xg.astype(BF16)
            else:
                s_ref[slot, j] = xg
        if dil > 1:
            rows = t // dil
            for r in range(dil):
                for j in range(ng):
                    o_ref[:, r * w + j * LANES:r * w + (j + 1) * LANES] = (
                        s_ref.at[slot, j][pl.ds(r, rows, stride=dil), :].astype(BF16))
            slot += 1


def _qkv(x, res, g, sh, sc, w_bf16, tables, seq, pieces):
    n, d = x.shape
    t = TOK_TILE
    tpb = seq // t
    c, a, b = tables
    tok = pl.BlockSpec((t, d), lambda i: (i, 0))
    vec = pl.BlockSpec((1, 1, d), lambda i: (i // tpb, 0, 0))
    tab = pl.BlockSpec((t, LANES), lambda i: (i % tpb, 0))
    in_specs = [tok]
    args = [x]
    if res is not None:
        in_specs += [tok, vec]
        args += [res[0], res[1]]
    in_specs += [pl.BlockSpec((1, d), lambda i: (0, 0)), vec, vec,
                 pl.BlockSpec(w_bf16.shape, lambda i: (0, 0)), tab, tab, tab]
    args += [g.reshape(1, d), sh, sc, w_bf16, c, a, b]
    widths = [dil * w * (2 if dup else 1) for (_, w, _, _, dil, dup) in pieces]
    out_shape = [jax.ShapeDtypeStruct((n // p[4], wd), BF16) for p, wd in zip(pieces, widths)]
    out_specs = [pl.BlockSpec((t // p[4], wd), lambda i: (i, 0)) for p, wd in zip(pieces, widths)]
    if res is not None:
        out_shape = [jax.ShapeDtypeStruct((n, d), F32)] + out_shape
        out_specs = [tok] + out_specs
    n_strided = max(1, sum(1 for p in pieces if p[4] > 1))
    kern = functools.partial(_qkv_kernel, pieces=pieces, has_res=res is not None)
    return pl.pallas_call(
        kern, out_shape=out_shape, grid=(n // t,), in_specs=in_specs, out_specs=out_specs,
        scratch_shapes=[pltpu.VMEM((n_strided, B_GW // LANES, t, LANES), F32)],
        compiler_params=_cparams(("arbitrary",)), name="qkv",
    )(*args)


def _attn_kernel(*refs, tq, win, n_pair, pairs_per_kgroup, seq_len, has_sink, want_lse):
    pos = 0
    if has_sink:
        sink_ref = refs[0]
        pos = 1
    q_ref, kp_ref, kc_ref, kn_ref, vp_ref, vc_ref, vn_ref = refs[pos:pos + 7]
    o_ref = refs[pos + 7]
    lse_ref = refs[pos + 8] if want_lse else None
    i = pl.program_id(2)
    qb = ATT_QB
    kw = qb + 2 * win
    k = jnp.concatenate([kp_ref[0], kc_ref[0], kn_ref[0]], axis=0)
    v = jnp.concatenate([vp_ref[0], vc_ref[0], vn_ref[0]], axis=0)
    low_half = lax.broadcasted_iota(I32, (1, LANES), 1) < HEAD_DIM
    r_io = lax.broadcasted_iota(I32, (qb, kw), 0)
    c_io = lax.broadcasted_iota(I32, (qb, kw), 1)
    band = (c_io >= r_io) & (c_io - r_io <= 2 * win)
    biases = []
    for j in range(tq // qb):
        kpos = i * tq - win + j * qb + c_io
        biases.append(jnp.where(band & (kpos >= 0) & (kpos < seq_len), 0.0, NEG))
    masked = []
    for gk in range(n_pair // pairs_per_kgroup):
        kg = k[:, gk * LANES:(gk + 1) * LANES]
        vg = v[:, gk * LANES:(gk + 1) * LANES]
        zero = jnp.zeros_like(kg)
        masked.append(((jnp.where(low_half, kg, zero), jnp.where(low_half, vg, zero)),
                       (jnp.where(low_half, zero, kg), jnp.where(low_half, zero, vg))))
    units = [(p, j) for p in range(n_pair) for j in range(tq // qb)]

    def scores(u):
        p, j = u
        qp = q_ref[0, j * qb:(j + 1) * qb, p * LANES:(p + 1) * LANES]
        return [lax.dot_general(qp, kh[j * qb:j * qb + kw], (((1,), (1,)), ((), ())),
                                preferred_element_type=F32) + biases[j]
                for kh, _ in masked[p // pairs_per_kgroup]]

    def finish(u, ss):
        p, j = u
        o_pair = None
        lses = []
        for half, s in enumerate(ss):
            vh = masked[p // pairs_per_kgroup][half][1]
            m = jnp.max(s, axis=-1, keepdims=True)
            if has_sink:
                sk = sink_ref[2 * p + half]
                m = jnp.maximum(m, sk)
            e = jnp.exp(s - m)
            l = jnp.sum(e, axis=-1, keepdims=True)
            if has_sink:
                l = l + jnp.exp(sk - m)
            o_h = jnp.dot(e.astype(BF16), vh[j * qb:j * qb + kw], preferred_element_type=F32) / l
            o_pair = o_h if o_pair is None else o_pair + o_h
            lses.append(m + jnp.log(l))
        o_ref[0, j * qb:(j + 1) * qb, p * LANES:(p + 1) * LANES] = o_pair.astype(o_ref.dtype)
        if want_lse:
            lse_ref[0, j * qb:(j + 1) * qb, p * LANES:(p + 1) * LANES] = jnp.where(
                low_half, lses[0], lses[1])

    pending = [scores(u) for u in units[:ATT_AHEAD]]
    for idx, u in enumerate(units):
        if idx + ATT_AHEAD < len(units):
            pending.append(scores(units[idx + ATT_AHEAD]))
        finish(u, pending.pop(0))


def _banded_attention(q, k, v, *, bsz, seq_n, dil, q_w, k_w,
                      win, pairs_per_kgroup, sink, want_lse, out_dtype):
    tq = ATT_TQ
    nt = seq_n // tq
    wpt = tq // win
    nwb = seq_n // win

    def qmap(b, r, i):
        return (b, i, r)

    def kprev(b, r, i):
        return (b, jnp.maximum(i * wpt - 1, 0), r)

    def knext(b, r, i):
        return (b, jnp.minimum((i + 1) * wpt, nwb - 1), r)

    in_specs = []
    args = []
    if sink is not None:
        in_specs.append(pl.BlockSpec(memory_space=pltpu.SMEM))
        args.append(sink)
    in_specs += [pl.BlockSpec((1, tq, q_w), qmap),
                 pl.BlockSpec((1, win, k_w), kprev), pl.BlockSpec((1, tq, k_w), qmap),
                 pl.BlockSpec((1, win, k_w), knext),
                 pl.BlockSpec((1, win, k_w), kprev), pl.BlockSpec((1, tq, k_w), qmap),
                 pl.BlockSpec((1, win, k_w), knext)]
    args += [q, k, k, k, v, v, v]
    out_shape = [jax.ShapeDtypeStruct((bsz, seq_n, dil * q_w), out_dtype)]
    out_specs = [pl.BlockSpec((1, tq, q_w), qmap)]
    if want_lse:
        out_shape.append(jax.ShapeDtypeStruct((bsz, seq_n, dil * q_w), F32))
        out_specs.append(pl.BlockSpec((1, tq, q_w), qmap))
    kern = functools.partial(_attn_kernel, tq=tq, win=win, n_pair=q_w // LANES,
                             pairs_per_kgroup=pairs_per_kgroup, seq_len=seq_n,
                             has_sink=sink is not None, want_lse=want_lse)
    return pl.pallas_call(
        kern, out_shape=out_shape, grid=(bsz, dil, nt), in_specs=in_specs, out_specs=out_specs,
        compiler_params=_cparams(("arbitrary", "arbitrary", "arbitrary")), name="attn",
    )(*args)


def _oproj_kernel(*refs, dils):
    t = TOK_TILE
    n_grp = len(dils)
    if n_grp == 1:
        o_refs, l_refs, rest = refs[:1], (), refs[1:]
    else:
        o_refs, l_refs, rest = refs[:n_grp], refs[n_grp:2 * n_grp], refs[2 * n_grp:]
    (x_ref, wo_ref, gt_ref, g_ref, sh_ref, sc_ref, wr_ref, br_ref,
     xo_ref, h_ref, at_ref, s_ref) = rest
    if n_grp == 1:
        o = o_refs[0][...]
    else:
        ng = B_GW // LANES

        def natural(ref, slot, dil):
            if dil == 1:
                return [ref[:, j * LANES:(j + 1) * LANES] for j in range(ng)]
            rows = t // dil
            for r in range(dil):
                for j in range(ng):
                    s_ref.at[slot, j][pl.ds(r, rows, stride=dil), :] = (
                        ref[:, r * B_GW + j * LANES:r * B_GW + (j + 1) * LANES])
            return [s_ref[slot, j] for j in range(ng)]

        os_ = [natural(o_refs[gi], 2 * gi, dils[gi]) for gi in range(n_grp)]
        ls_ = [natural(l_refs[gi], 2 * gi + 1, dils[gi]) for gi in range(n_grp)]
        cols = []
        for j in range(ng):
            ls = [ls_[gi][j] for gi in range(n_grp)]
            mx = ls[0]
            for v in ls[1:]:
                mx = jnp.maximum(mx, v)
            es = [jnp.exp(v - mx) for v in ls]
            den = es[0]
            for v in es[1:]:
                den = den + v
            oj = (es[0] / den) * os_[0][j]
            for gi in range(1, n_grp):
                oj = oj + (es[gi] / den) * os_[gi][j]
            cols.append(oj.astype(BF16))
        o = jnp.concatenate(cols, axis=1)
    y = jnp.dot(o, wo_ref[...], preferred_element_type=F32)
    x = x_ref[...] + gt_ref[0] * y
    xo_ref[...] = x
    h = _norm_mod(x, g_ref[...], sh_ref[0], sc_ref[0])
    hb = h.astype(BF16)
    h_ref[...] = hb
    logits = jnp.dot(hb, wr_ref[...], preferred_element_type=F32) + br_ref[...]
    lane = lax.broadcasted_iota(I32, logits.shape, 1)
    logits = jnp.where(lane < N_EXPERTS, logits, NEG)
    mx = jnp.max(logits, axis=-1, keepdims=True)
    ex = jnp.exp(logits - mx)
    aff = ex / jnp.sum(ex, axis=-1, keepdims=True)
    at_ref[...] = aff.T[:N_EXPERTS, :]


def _oproj(os_, ls_, dils, x, wo_bf16, gt, g, sh, sc, wr_pad, br_pad, seq):
    n, d = x.shape
    t = TOK_TILE
    tpb = seq // t
    n_grp = len(os_)
    tok = pl.BlockSpec((t, d), lambda i: (i, 0))
    gspecs = [pl.BlockSpec((t // dil, o.shape[1]), lambda i: (i, 0)) for o, dil in zip(os_, dils)]
    vec = pl.BlockSpec((1, 1, d), lambda i: (i // tpb, 0, 0))
    row = pl.BlockSpec((1, d), lambda i: (0, 0))
    in_specs = gspecs + (gspecs if n_grp > 1 else []) + [
        tok, pl.BlockSpec(wo_bf16.shape, lambda i: (0, 0)), vec, row, vec, vec,
        pl.BlockSpec(wr_pad.shape, lambda i: (0, 0)), pl.BlockSpec((1, LANES), lambda i: (0, 0))]
    args = list(os_) + (list(ls_) if n_grp > 1 else []) + [
        x, wo_bf16, gt, g.reshape(1, d), sh, sc, wr_pad, br_pad]
    out_shape = [jax.ShapeDtypeStruct((n, d), F32), jax.ShapeDtypeStruct((n, d), BF16),
                 jax.ShapeDtypeStruct((N_EXPERTS, n), F32)]
    out_specs = [tok, tok, pl.BlockSpec((N_EXPERTS, t), lambda i: (0, i))]
    return pl.pallas_call(
        functools.partial(_oproj_kernel, dils=tuple(dils)),
        out_shape=out_shape, grid=(n // t,), in_specs=in_specs, out_specs=out_specs,
        scratch_shapes=[pltpu.VMEM((2 * n_grp, B_GW // LANES, t, LANES), F32)],
        compiler_params=_cparams(("arbitrary",)), name="oproj",
    )(*args)


def _thr_kernel(a_ref, tau_ref, need_ref, *, cap):
    bits = pltpu.bitcast(a_ref[...], I32)

    def body(it, tau):
        cand = tau | jnp.left_shift(jnp.int32(1), 30 - it)
        cnt = jnp.sum(jnp.where(bits >= cand[:, :1], 1.0, 0.0), axis=1, keepdims=True)
        return jnp.where(cnt >= cap, cand, tau)

    tau = lax.fori_loop(0, 31, body, jnp.zeros((N_EXPERTS, LANES), I32))
    cgt = jnp.sum(jnp.where(bits > tau[:, :1], 1.0, 0.0), axis=1, keepdims=True)
    tau_ref[...] = tau
    need_ref[...] = jnp.broadcast_to(cap - cgt, (N_EXPERTS, LANES))


def _thresholds(aff_t, cap):
    e, n = aff_t.shape
    vm = pl.BlockSpec(memory_space=pltpu.VMEM)
    return pl.pallas_call(
        functools.partial(_thr_kernel, cap=float(cap)),
        out_shape=[jax.ShapeDtypeStruct((e, LANES), I32), jax.ShapeDtypeStruct((e, LANES), F32)],
        in_specs=[vm], out_specs=[vm, vm],
        compiler_params=pltpu.CompilerParams(vmem_limit_bytes=VMEM_LIMIT), name="thr",
    )(aff_t)


def _sel_kernel(at_ref, tau_ref, need_ref, kp_ref, cnt_ref, run_ref):
    @pl.when(pl.program_id(0) == 0)
    def _():
        run_ref[...] = jnp.zeros_like(run_ref)

    c = MOE_CHUNK
    tau = tau_ref[...]
    need = need_ref[...]
    r_io = lax.broadcasted_iota(I32, (c, c), 0)
    c_io = lax.broadcasted_iota(I32, (c, c), 1)
    upper = jnp.where(r_io < c_io, 1.0, 0.0).astype(BF16)

    lane16 = lax.broadcasted_iota(I32, (N_EXPERTS, LANES), 1)

    cnt_ref[0] = jnp.zeros((N_EXPERTS, LANES), I32)

    def chunk(ci, base):
        cnt_ref[0] = jnp.where(lane16 == ci, base.astype(I32), cnt_ref[0])
        off = pl.multiple_of(ci * c, c)
        a = at_ref[:, pl.ds(off, c)]
        bits = pltpu.bitcast(a, I32)
        gt = bits > tau
        eq = bits == tau
        eqf = jnp.where(eq, 1.0, 0.0)
        eq_before = jnp.dot(eqf.astype(BF16), upper, preferred_element_type=F32) + run_ref[...]
        sel = gt | (eq & (eq_before < need))
        self_ = jnp.where(sel, 1.0, 0.0)
        cum = jnp.dot(self_.astype(BF16), upper, preferred_element_type=F32)
        kp_ref[:, pl.ds(off, c)] = jnp.where(sel, cum + base + 1.0, 0.0)
        run_ref[...] = run_ref[...] + jnp.sum(eqf, axis=1, keepdims=True)
        return base + jnp.sum(self_, axis=1, keepdims=True)

    total = lax.fori_loop(0, N_CHUNK, chunk, jnp.zeros((N_EXPERTS, LANES), F32))
    cnt_ref[0] = jnp.where(lane16 == N_CHUNK, total.astype(I32), cnt_ref[0])


def _select(aff_t, tau, need):
    e, n = aff_t.shape
    nt = n // MOE_TILE
    full = pl.BlockSpec((e, LANES), lambda i: (0, 0))
    return pl.pallas_call(
        _sel_kernel,
        out_shape=[jax.ShapeDtypeStruct((e, n), F32),
                   jax.ShapeDtypeStruct((nt, e, LANES), I32)],
        grid=(nt,),
        in_specs=[pl.BlockSpec((e, MOE_TILE), lambda i: (0, i)), full, full],
        out_specs=[pl.BlockSpec((e, MOE_TILE), lambda i: (0, i)),
                   pl.BlockSpec((1, e, LANES), lambda i: (i, 0, 0))],
        scratch_shapes=[pltpu.VMEM((e, LANES), F32)],
        compiler_params=_cparams(("arbitrary",)), name="sel",
    )(aff_t, tau, need)


def _swiglu(x, wg_ref, wu_ref, wd_ref):
    hg = jnp.dot(x, wg_ref[0], preferred_element_type=F32)
    hu = jnp.dot(x, wu_ref[0], preferred_element_type=F32)
    hid = (jax.nn.silu(hg) * hu).astype(BF16)
    return jnp.dot(hid, wd_ref[0], preferred_element_type=F32)


def _ffn_kernel(cb_ref, kp_ref, at_ref, h_ref, wg_ref, wu_ref, wd_ref, o_ref,
                xy_ref, x_ref, yg_ref):
    t_id = pl.program_id(0)
    e_id = pl.program_id(1)

    def cb(ee, lane):
        return cb_ref[(t_id * N_EXPERTS + ee) * CB_STRIDE + lane]

    m = cb(e_id, N_CHUNK)
    m_max = cb(0, N_CHUNK)
    for ee in range(1, N_EXPERTS):
        m_max = jnp.maximum(m_max, cb(ee, N_CHUNK))
    fast = m_max <= MOE_CAP

    def gate_col(first_slot, rows):
        sub = lax.broadcasted_iota(I32, (rows, MOE_TILE), 0) + (first_slot + 1)
        hit = kp_ref[pl.ds(e_id, 1), :] == sub.astype(F32)
        return jnp.sum(jnp.where(hit, at_ref[pl.ds(e_id, 1), :], 0.0), axis=1, keepdims=True)

    cpw = MOE_WIN // MOE_CHUNK

    def win_rounds(w):
        nr = jnp.int32(0)
        for ee in range(N_EXPERTS):
            a = (cb(ee, cpw * w) >> 4) << 4
            nr = jnp.maximum(nr, (cb(ee, cpw * w + cpw) - a + MOE_SLOT - 1) // MOE_SLOT)
        return nr

    def slot_rows(w, r):
        return [jnp.minimum(((cb(ee, cpw * w) >> 4) << 4) + r * MOE_SLOT, MOE_CAP)
                for ee in range(N_EXPERTS)]

    def sel_matrix(w, rows):
        kpw = kp_ref[:, pl.ds(pl.multiple_of(w * MOE_WIN, MOE_WIN), MOE_WIN)]
        sub = lax.broadcasted_iota(I32, (MOE_SLOT, MOE_WIN), 0).astype(F32)
        parts = [jnp.where(kpw[ee:ee + 1, :] == sub + (rows[ee] + 1).astype(F32), 1.0, 0.0).astype(BF16)
                 for ee in range(N_EXPERTS)]
        return jnp.concatenate(parts, axis=0)

    def buf_off(ee, row):
        return pl.multiple_of(ee * MOE_CAP + row, 16)

    @pl.when(fast & (e_id == 0))
    def _dispatch():
        xy_ref[...] = jnp.zeros_like(xy_ref)

        def win(w, c1):
            hw = h_ref[pl.ds(pl.multiple_of(w * MOE_WIN, MOE_WIN), MOE_WIN), :]

            def rnd(r, c2):
                rows = slot_rows(w, r)
                x = jnp.dot(sel_matrix(w, rows), hw, preferred_element_type=F32)
                for ee in range(N_EXPERTS):
                    xy_ref[pl.ds(buf_off(ee, rows[ee]), MOE_SLOT), :] += (
                        x[ee * MOE_SLOT:(ee + 1) * MOE_SLOT, :].astype(BF16))
                return c2

            lax.fori_loop(0, win_rounds(w), rnd, 0)
            return c1

        lax.fori_loop(0, N_WIN, win, 0)

    def ffn_rows(first, rows):
        r0 = pl.multiple_of(e_id * MOE_CAP + first, 16)
        ye = _swiglu(xy_ref[pl.ds(r0, rows), :], wg_ref, wu_ref, wd_ref)
        xy_ref[pl.ds(r0, rows), :] = (ye * gate_col(first, rows)).astype(BF16)

    @pl.when(fast & (m > 0))
    def _():
        ffn_rows(0, MOE_HEAD)

    @pl.when(fast & (m > MOE_HEAD))
    def _():
        ffn_rows(MOE_HEAD, MOE_CAP - MOE_HEAD)

    @pl.when(fast & (e_id == N_EXPERTS - 1))
    def _combine():
        def win(w, c1):
            w0 = pl.multiple_of(w * MOE_WIN, MOE_WIN)
            o_ref[pl.ds(w0, MOE_WIN), :] = jnp.zeros((MOE_WIN, o_ref.shape[1]), F32)

            def rnd(r, c2):
                rows = slot_rows(w, r)
                y = jnp.concatenate([xy_ref[pl.ds(buf_off(ee, rows[ee]), MOE_SLOT), :]
                                     for ee in range(N_EXPERTS)], axis=0)
                o_ref[pl.ds(w0, MOE_WIN), :] += lax.dot_general(
                    sel_matrix(w, rows), y, (((0,), (0,)), ((), ())), preferred_element_type=F32)
                return c2

            lax.fori_loop(0, win_rounds(w), rnd, 0)
            return c1

        lax.fori_loop(0, N_WIN, win, 0)

    @pl.when(jnp.logical_not(fast))
    def _general():
        @pl.when(e_id == 0)
        def _():
            o_ref[...] = jnp.zeros_like(o_ref)

        sub_sb = lax.broadcasted_iota(I32, (MOE_SB, MOE_TILE), 0).astype(F32)

        def superblock(si, carry):
            lo = si * MOE_SB
            kp_all = kp_ref[pl.ds(e_id, 1), :]
            p = jnp.where(kp_all == sub_sb + (lo + 1).astype(F32), 1.0, 0.0).astype(BF16)
            x_ref[...] = jnp.dot(p, h_ref[...], preferred_element_type=F32).astype(BF16)
            yg_ref[...] = jnp.zeros_like(yg_ref)

            def block(bi, c2):
                r0 = pl.multiple_of(bi * MOE_BLK, MOE_BLK)
                ye = _swiglu(x_ref[pl.ds(r0, MOE_BLK), :], wg_ref, wu_ref, wd_ref)
                yg_ref[pl.ds(r0, MOE_BLK), :] = (ye * gate_col(lo + r0, MOE_BLK)).astype(BF16)
                return c2

            nblk = jnp.minimum((m - lo + MOE_BLK - 1) // MOE_BLK, MOE_SB // MOE_BLK)
            lax.fori_loop(0, nblk, block, 0)
            o_ref[...] += lax.dot_general(p, yg_ref[...], (((0,), (0,)), ((), ())),
                                          preferred_element_type=F32)
            return carry

        lax.fori_loop(0, (m + MOE_SB - 1) // MOE_SB, superblock, 0)


def _moe_ffn(h, kp, aff_t, cb, wg, wu, wd):
    n, d = h.shape
    nt = n // MOE_TILE
    f = wg.shape[2]
    return pl.pallas_call(
        _ffn_kernel,
        out_shape=jax.ShapeDtypeStruct((n, d), F32),
        grid=(nt, N_EXPERTS),
        in_specs=[pl.BlockSpec(memory_space=pltpu.SMEM),
                  pl.BlockSpec((N_EXPERTS, MOE_TILE), lambda t, e: (0, t)),
                  pl.BlockSpec((N_EXPERTS, MOE_TILE), lambda t, e: (0, t)),
                  pl.BlockSpec((MOE_TILE, d), lambda t, e: (t, 0)),
                  pl.BlockSpec((1, d, f), lambda t, e: (e, 0, 0)),
                  pl.BlockSpec((1, d, f), lambda t, e: (e, 0, 0)),
                  pl.BlockSpec((1, f, d), lambda t, e: (e, 0, 0))],
        out_specs=pl.BlockSpec((MOE_TILE, d), lambda t, e: (t, 0), pipeline_mode=pl.Buffered(1)),
        scratch_shapes=[pltpu.VMEM((N_EXPERTS * MOE_CAP + MOE_SLOT, d), BF16),
                        pltpu.VMEM((MOE_SB, d), BF16), pltpu.VMEM((MOE_SB, d), BF16)],
        compiler_params=_cparams(("arbitrary", "arbitrary")), name="ffn",
    )(cb, kp, aff_t, h, wg, wu, wd)


def _moe(h, aff_t, wg, wu, wd):
    n = h.shape[0]
    cap = max(1, EC_CAPACITY * n // N_EXPERTS)
    tau, need = _thresholds(aff_t, cap)
    kp, cb = _select(aff_t, tau, need)
    return _moe_ffn(h, kp, aff_t, cb[:, :, :CB_STRIDE].reshape(-1), wg, wu, wd)


def _final_kernel(x_ref, m_ref, gt_ref, g_ref, o_ref):
    x = x_ref[...] + gt_ref[0] * m_ref[...]
    ms = jnp.mean(x * x, axis=-1, keepdims=True)
    o_ref[...] = (x * lax.rsqrt(ms + EPS)) * g_ref[...]


def _final(x, moe, gt, g, seq):
    n, d = x.shape
    t = TOK_TILE
    tpb = seq // t
    tok = pl.BlockSpec((t, d), lambda i: (i, 0))
    return pl.pallas_call(
        _final_kernel, out_shape=jax.ShapeDtypeStruct((n, d), F32), grid=(n // t,),
        in_specs=[tok, tok, pl.BlockSpec((1, 1, d), lambda i: (i // tpb, 0, 0)),
                  pl.BlockSpec((1, d), lambda i: (0, 0))],
        out_specs=tok, compiler_params=_cparams(("arbitrary",)), name="final",
    )(x, moe, gt, g.reshape(1, d))


def _trunk(x3, mods, p):
    bsz, seq, d = x3.shape
    n = bsz * seq
    x = x3.reshape(n, d)
    tables = _rope_tables(seq)
    scale = HEAD_DIM ** -0.5
    sh1, sc1, gt1, sh2, sc2, gt2 = [m[:, None, :] for m in jnp.split(mods[0], 6, axis=-1)]
    qw, kw = A_HEADS * HEAD_DIM, A_KV_HEADS * HEAD_DIM
    pieces = ((0, qw, True, scale, 1, False), (qw, kw, True, 1.0, 1, True),
              (qw + kw, kw, False, 1.0, 1, True))
    q, k, v = _qkv(x, None, p["g_mix"][0], sh1, sc1, p["w_qkv_a"], tables, seq, pieces)
    (o,) = _banded_attention(
        q.reshape(bsz, seq, qw), k.reshape(bsz, seq, 2 * kw), v.reshape(bsz, seq, 2 * kw),
        bsz=bsz, seq_n=seq, dil=1, q_w=qw, k_w=2 * kw, win=A_WINDOW,
        pairs_per_kgroup=A_HEADS // A_KV_HEADS // 2, sink=p["sink_a"], want_lse=False,
        out_dtype=BF16)
    x, h, aff_t = _oproj([o.reshape(n, qw)], None, (1,), x, p["w_o_a"], gt1, p["g_ffn"][0],
                         sh2, sc2, p["w_router"][0], p["b_router"][0], seq)
    moe = _moe(h, aff_t, p["w_gate"][0], p["w_up"][0], p["w_down"][0])
    gt2_prev = gt2
    sh1, sc1, gt1, sh2, sc2, gt2 = [m[:, None, :] for m in jnp.split(mods[1], 6, axis=-1)]
    ng = len(B_GROUPS)
    dils = tuple(dil for _, dil in B_GROUPS)
    pieces = tuple((part * ng * B_GW + gi * B_GW, B_GW, part < 2, scale if part == 0 else 1.0, dils[gi], False)
                   for gi in range(ng) for part in range(3))
    res = _qkv(x, (moe, gt2_prev), p["g_mix"][1], sh1, sc1, p["w_qkv_b"], tables, seq, pieces)
    x = res[0]
    outs, lses = [], []
    for gi, (window, dil) in enumerate(B_GROUPS):
        sn = seq // dil
        qg, kg, vg = [r.reshape(bsz, sn, dil * B_GW) for r in res[1 + 3 * gi:4 + 3 * gi]]
        og, lg = _banded_attention(
            qg, kg, vg, bsz=bsz, seq_n=sn, dil=dil, q_w=B_GW, k_w=B_GW,
            win=(window // 2) // dil, pairs_per_kgroup=1, sink=None, want_lse=True,
            out_dtype=F32)
        outs.append(og.reshape(bsz * sn, dil * B_GW))
        lses.append(lg.reshape(bsz * sn, dil * B_GW))
    x, h, aff_t = _oproj(outs, lses, dils, x, p["w_o_b"], gt1, p["g_ffn"][1],
                         sh2, sc2, p["w_router"][1], p["b_router"][1], seq)
    moe = _moe(h, aff_t, p["w_gate"][1], p["w_up"][1], p["w_down"][1])
    y = _final(x, moe, gt2, p["g_final"], seq)
    return y.reshape(bsz, seq, d)


def kernel(x_prompt, x_sample, c_prompt, c_sample, w_ada, b_ada, g_mix, g_ffn, w_qkv_a, sink_a, w_o_a, w_qkv_b, w_o_b, w_router, b_router, w_gate, w_up, w_down, g_final):
    bp = c_prompt.shape[0]
    bs = c_sample.shape[0]
    rows = -(-(bp + bs) // SUBLANES) * SUBLANES
    c_all = jnp.zeros((rows, D_MODEL), F32).at[:bp].set(c_prompt).at[bp:bp + bs].set(c_sample)
    mods = _ada(c_all, w_ada, b_ada)
    depth = w_ada.shape[0]
    wr_pad = jnp.zeros((depth, D_MODEL, LANES), F32).at[:, :, :N_EXPERTS].set(w_router).astype(BF16)
    br_pad = jnp.zeros((depth, 1, LANES), F32).at[:, 0, :N_EXPERTS].set(b_router)
    p = dict(
        g_mix=g_mix, g_ffn=g_ffn, g_final=g_final, sink_a=sink_a[0],
        w_qkv_a=w_qkv_a[0].astype(BF16), w_o_a=w_o_a[0].astype(BF16),
        w_qkv_b=w_qkv_b[0].astype(BF16), w_o_b=w_o_b[0].astype(BF16),
        w_router=wr_pad, b_router=br_pad,
        w_gate=w_gate.astype(BF16), w_up=w_up.astype(BF16), w_down=w_down.astype(BF16),
    )
    y_prompt = _trunk(x_prompt, mods[:, :bp], p)
    y_sample = _trunk(x_sample, mods[:, bp:bp + bs], p)
    return (y_prompt, y_sample)
```

```python
import functools

import jax
import jax.numpy as jnp
from jax import lax
from jax.experimental import pallas as pl
from jax.experimental.pallas import tpu as pltpu

F32 = jnp.float32
BF16 = jnp.bfloat16
I32 = jnp.int32

D_MODEL = 1024
HEAD_DIM = 64
A_HEADS = 16
A_KV_HEADS = 4
A_WINDOW = 128
B_GROUPS = ((128, 1), (512, 4), (2048, 16))
B_HEADS = 8
B_GW = B_HEADS * HEAD_DIM
N_EXPERTS = 16
EXPERT_FF = 1024
EC_CAPACITY = 2
ROPE_THETA = 10000.0
EPS = 1e-6
NEG = -1e30

LANES = 128
SUBLANES = 8
VMEM_LIMIT = 56 * 1024 * 1024

TOK_TILE = 512
TOK_TILE_A = 1024
FINAL_TILE = 2048
OPROJ_SPLIT = 2
ATT_TQ = 512
ATT_QB = 128
ATT_AHEAD = 3
MOE_TILE = 2048
MOE_CHUNK = 128
MOE_SB = 384
MOE_CAP = 384
MOE_WIN = 256
MOE_SLOT = 64
N_WIN = MOE_TILE // MOE_WIN
CB_STRIDE = 32
MOE_BLK = 128
MOE_HEAD = 288
N_CHUNK = MOE_TILE // MOE_CHUNK


def _cparams(sem):
    return pltpu.CompilerParams(dimension_semantics=sem, vmem_limit_bytes=VMEM_LIMIT)


def _norm_mod(x, g, sh, sc):
    ms = jnp.mean(x * x, axis=-1, keepdims=True)
    y = (x * lax.rsqrt(ms + EPS)) * g
    return y * (1.0 + sc) + sh


def _ada_kernel(c_ref, w_ref, b_ref, o_ref):
    s = jax.nn.silu(c_ref[...])
    o_ref[0] = jnp.dot(s, w_ref[0], preferred_element_type=F32,
                       precision=lax.Precision.HIGHEST) + b_ref[0]


def _ada(c_all, w_ada, b_ada):
    depth, d, n6 = w_ada.shape
    rows = c_all.shape[0]
    tn = 1536
    return pl.pallas_call(
        _ada_kernel,
        out_shape=jax.ShapeDtypeStruct((depth, rows, n6), F32),
        grid=(depth, n6 // tn),
        in_specs=[
            pl.BlockSpec((rows, d), lambda l, j: (0, 0)),
            pl.BlockSpec((1, d, tn), lambda l, j: (l, 0, j)),
            pl.BlockSpec((1, 1, tn), lambda l, j: (l, 0, j)),
        ],
        out_specs=pl.BlockSpec((1, rows, tn), lambda l, j: (l, 0, j)),
        compiler_params=_cparams(("arbitrary", "arbitrary")),
        name="ada",
    )(c_all, w_ada, b_ada.reshape(depth, 1, n6))


def _rope_tables(seq):
    half = HEAD_DIM // 2
    inv = ROPE_THETA ** (-jnp.arange(half, dtype=F32) / half)
    ang = jnp.arange(seq).astype(F32)[:, None] * inv[None, :]
    cos = jnp.cos(ang)
    sin = jnp.sin(ang)
    zero = jnp.zeros_like(sin)
    c = jnp.concatenate([cos, cos, cos, cos], axis=1)
    a = jnp.concatenate([-sin, zero, -sin, zero], axis=1)
    b = jnp.concatenate([zero, sin, zero, sin], axis=1)
    return c, a, b


def _rope_group(xg, c, a, b):
    return xg * c + pltpu.roll(xg, LANES - HEAD_DIM // 2, 1) * a + pltpu.roll(xg, HEAD_DIM // 2, 1) * b


def _qkv_kernel(*refs, pieces, has_res):
    t = refs[0].shape[0]
    if has_res:
        x_ref, m_ref, gt_ref, g_ref, sh_ref, sc_ref, w_ref, c_ref, a_ref, b_ref, xo_ref = refs[:11]
        out_refs = refs[11:11 + len(pieces)]
        x = x_ref[...] + gt_ref[0] * m_ref[...]
        xo_ref[...] = x
    else:
        x_ref, g_ref, sh_ref, sc_ref, w_ref, c_ref, a_ref, b_ref = refs[:8]
        out_refs = refs[8:8 + len(pieces)]
        x = x_ref[...]
    s_ref = refs[-1]
    h = _norm_mod(x, g_ref[...], sh_ref[0], sc_ref[0]).astype(BF16)
    acc = jnp.dot(h, w_ref[...], preferred_element_type=F32)
    c = c_ref[...]
    a = a_ref[...]
    b = b_ref[...]
    slot = 0
    low_half = lax.broadcasted_iota(I32, (1, LANES), 1) < HEAD_DIM
    for (c0, w, rope, scale, dil, dup), o_ref in zip(pieces, out_refs):
        ng = w // LANES
        for j in range(ng):
            xg = acc[:, c0 + j * LANES:c0 + (j + 1) * LANES]
            if rope:
                xg = _rope_group(xg, c, a, b)
            if scale != 1.0:
                xg = xg * scale
            if dup:
                sw = pltpu.roll(xg, HEAD_DIM, 1)
                o_ref[:, 2 * j * LANES:(2 * j + 1) * LANES] = jnp.where(low_half, xg, sw).astype(BF16)
                o_ref[:, (2 * j + 1) * LANES:(2 * j + 2) * LANES] = jnp.where(low_half, sw, xg).astype(BF16)
            elif dil == 1:
                o_ref[:, j * LANES:(j + 1) * LANES] = xg.astype(BF16)
            else:
                s_ref[slot, j] = xg
        if dil > 1:
            rows = t // dil
            for r in range(dil):
                for j in range(ng):
                    o_ref[:, r * w + j * LANES:r * w + (j + 1) * LANES] = (
                        s_ref.at[slot, j][pl.ds(r, rows, stride=dil), :].astype(BF16))
            slot += 1


def _qkv(x, res, g, sh, sc, w_bf16, tables, seq, pieces, t):
    n, d = x.shape
    tpb = seq // t
    c, a, b = tables
    tok = pl.BlockSpec((t, d), lambda i: (i, 0))
    vec = pl.BlockSpec((1, 1, d), lambda i: (i // tpb, 0, 0))
    tab = pl.BlockSpec((t, LANES), lambda i: (i % tpb, 0))
    in_specs = [tok]
    args = [x]
    if res is not None:
        in_specs += [tok, vec]
        args += [res[0], res[1]]
    in_specs += [pl.BlockSpec((1, d), lambda i: (0, 0)), vec, vec,
                 pl.BlockSpec(w_bf16.shape, lambda i: (0, 0)), tab, tab, tab]
    args += [g.reshape(1, d), sh, sc, w_bf16, c, a, b]
    widths = [dil * w * (2 if dup else 1) for (_, w, _, _, dil, dup) in pieces]
    out_shape = [jax.ShapeDtypeStruct((n // p[4], wd), BF16) for p, wd in zip(pieces, widths)]
    out_specs = [pl.BlockSpec((t // p[4], wd), lambda i: (i, 0)) for p, wd in zip(pieces, widths)]
    if res is not None:
        out_shape = [jax.ShapeDtypeStruct((n, d), F32)] + out_shape
        out_specs = [tok] + out_specs
    n_strided = max(1, sum(1 for p in pieces if p[4] > 1))
    kern = functools.partial(_qkv_kernel, pieces=pieces, has_res=res is not None)
    return pl.pallas_call(
        kern, out_shape=out_shape, grid=(n // t,), in_specs=in_specs, out_specs=out_specs,
        scratch_shapes=[pltpu.VMEM((n_strided, B_GW // LANES, t, LANES), F32)],
        compiler_params=_cparams(("arbitrary",)), name="qkv",
    )(*args)


def _attn_kernel(*refs, tq, win, n_pair, pairs_per_kgroup, seq_len, has_sink, want_lse):
    pos = 0
    if has_sink:
        sink_ref = refs[0]
        pos = 1
    q_ref, kp_ref, kc_ref, kn_ref, vp_ref, vc_ref, vn_ref = refs[pos:pos + 7]
    o_ref = refs[pos + 7]
    lse_ref = refs[pos + 8] if want_lse else None
    i = pl.program_id(2)
    qb = ATT_QB
    kw = qb + 2 * win
    k = jnp.concatenate([kp_ref[0], kc_ref[0], kn_ref[0]], axis=0)
    v = jnp.concatenate([vp_ref[0], vc_ref[0], vn_ref[0]], axis=0)
    low_half = lax.broadcasted_iota(I32, (1, LANES), 1) < HEAD_DIM
    r_io = lax.broadcasted_iota(I32, (qb, kw), 0)
    c_io = lax.broadcasted_iota(I32, (qb, kw), 1)
    band = (c_io >= r_io) & (c_io - r_io <= 2 * win)
    biases = []
    for j in range(tq // qb):
        kpos = i * tq - win + j * qb + c_io
        biases.append(jnp.where(band & (kpos >= 0) & (kpos < seq_len), 0.0, NEG))
    masked = []
    for gk in range(n_pair // pairs_per_kgroup):
        kg = k[:, gk * LANES:(gk + 1) * LANES]
        vg = v[:, gk * LANES:(gk + 1) * LANES]
        zero = jnp.zeros_like(kg)
        masked.append(((jnp.where(low_half, kg, zero), jnp.where(low_half, vg, zero)),
                       (jnp.where(low_half, zero, kg), jnp.where(low_half, zero, vg))))
    units = [(p, j) for p in range(n_pair) for j in range(tq // qb)]

    def scores(u):
        p, j = u
        qp = q_ref[0, j * qb:(j + 1) * qb, p * LANES:(p + 1) * LANES]
        return [lax.dot_general(qp, kh[j * qb:j * qb + kw], (((1,), (1,)), ((), ())),
                                preferred_element_type=F32) + biases[j]
                for kh, _ in masked[p // pairs_per_kgroup]]

    def finish(u, ss):
        p, j = u
        o_pair = None
        lses = []
        for half, s in enumerate(ss):
            vh = masked[p // pairs_per_kgroup][half][1]
            m = jnp.max(s, axis=-1, keepdims=True)
            if has_sink:
                sk = sink_ref[2 * p + half]
                m = jnp.maximum(m, sk)
            e = jnp.exp(s - m)
            l = jnp.sum(e, axis=-1, keepdims=True)
            if has_sink:
                l = l + jnp.exp(sk - m)
            o_h = jnp.dot(e.astype(BF16), vh[j * qb:j * qb + kw], preferred_element_type=F32) / l
            o_pair = o_h if o_pair is None else o_pair + o_h
            lses.append(m + jnp.log(l))
        o_ref[0, j * qb:(j + 1) * qb, p * LANES:(p + 1) * LANES] = o_pair.astype(o_ref.dtype)
        if want_lse:
            lse_ref[0, j * qb:(j + 1) * qb, p * LANES:(p + 1) * LANES] = jnp.where(
                low_half, lses[0], lses[1])

    pending = [scores(u) for u in units[:ATT_AHEAD]]
    for idx, u in enumerate(units):
        if idx + ATT_AHEAD < len(units):
            pending.append(scores(units[idx + ATT_AHEAD]))
        finish(u, pending.pop(0))


def _banded_attention(q, k, v, *, bsz, seq_n, dil, q_w, k_w,
                      win, pairs_per_kgroup, sink, want_lse, out_dtype):
    tq = min(ATT_TQ, seq_n)
    nt = seq_n // tq
    wpt = tq // win
    nwb = seq_n // win

    def qmap(b, r, i):
        return (b, i, r)

    def kprev(b, r, i):
        return (b, jnp.maximum(i * wpt - 1, 0), r)

    def knext(b, r, i):
        return (b, jnp.minimum((i + 1) * wpt, nwb - 1), r)

    in_specs = []
    args = []
    if sink is not None:
        in_specs.append(pl.BlockSpec(memory_space=pltpu.SMEM))
        args.append(sink)
    in_specs += [pl.BlockSpec((1, tq, q_w), qmap),
                 pl.BlockSpec((1, win, k_w), kprev), pl.BlockSpec((1, tq, k_w), qmap),
                 pl.BlockSpec((1, win, k_w), knext),
                 pl.BlockSpec((1, win, k_w), kprev), pl.BlockSpec((1, tq, k_w), qmap),
                 pl.BlockSpec((1, win, k_w), knext)]
    args += [q, k, k, k, v, v, v]
    out_shape = [jax.ShapeDtypeStruct((bsz, seq_n, dil * q_w), out_dtype)]
    out_specs = [pl.BlockSpec((1, tq, q_w), qmap)]
    if want_lse:
        out_shape.append(jax.ShapeDtypeStruct((bsz, seq_n, dil * q_w), F32))
        out_specs.append(pl.BlockSpec((1, tq, q_w), qmap))
    kern = functools.partial(_attn_kernel, tq=tq, win=win, n_pair=q_w // LANES,
                             pairs_per_kgroup=pairs_per_kgroup, seq_len=seq_n,
                             has_sink=sink is not None, want_lse=want_lse)
    return pl.pallas_call(
        kern, out_shape=out_shape, grid=(bsz, dil, nt), in_specs=in_specs, out_specs=out_specs,
        compiler_params=_cparams(("arbitrary", "arbitrary", "arbitrary")), name="attn",
    )(*args)


def _oproj_kernel(*refs, dils):
    t = refs[0].shape[0] * dils[0]
    n_grp = len(dils)
    if n_grp == 1:
        o_refs, l_refs, rest = refs[:1], (), refs[1:]
    else:
        o_refs, l_refs, rest = refs[:n_grp], refs[n_grp:2 * n_grp], refs[2 * n_grp:]
    (x_ref, wo_ref, gt_ref, g_ref, sh_ref, sc_ref, wr_ref, br_ref,
     xo_ref, h_ref, at_ref, s_ref) = rest
    if n_grp == 1:
        o = o_refs[0][...]
    else:
        ng = B_GW // LANES

        def natural(ref, slot, dil):
            if dil == 1:
                return [ref[:, j * LANES:(j + 1) * LANES].astype(F32) for j in range(ng)]
            rows = t // dil
            for r in range(dil):
                for j in range(ng):
                    s_ref.at[slot, j][pl.ds(r, rows, stride=dil), :] = (
                        ref[:, r * B_GW + j * LANES:r * B_GW + (j + 1) * LANES].astype(F32))
            return [s_ref[slot, j] for j in range(ng)]

        os_ = [natural(o_refs[gi], 2 * gi, dils[gi]) for gi in range(n_grp)]
        ls_ = [natural(l_refs[gi], 2 * gi + 1, dils[gi]) for gi in range(n_grp)]
        cols = []
        for j in range(ng):
            ls = [ls_[gi][j] for gi in range(n_grp)]
            mx = ls[0]
            for v in ls[1:]:
                mx = jnp.maximum(mx, v)
            es = [jnp.exp(v - mx) for v in ls]
            den = es[0]
            for v in es[1:]:
                den = den + v
            oj = (es[0] / den) * os_[0][j]
            for gi in range(1, n_grp):
                oj = oj + (es[gi] / den) * os_[gi][j]
            cols.append(oj.astype(BF16))
        o = jnp.concatenate(cols, axis=1)
    hr = t // OPROJ_SPLIT
    ys = [jnp.dot(o[i * hr:(i + 1) * hr], wo_ref[...], preferred_element_type=F32)
          for i in range(OPROJ_SPLIT)]
    hbs = []
    for i in range(OPROJ_SPLIT):
        x = x_ref[i * hr:(i + 1) * hr, :] + gt_ref[0] * ys[i]
        xo_ref[i * hr:(i + 1) * hr, :] = x
        hb = _norm_mod(x, g_ref[...], sh_ref[0], sc_ref[0]).astype(BF16)
        h_ref[i * hr:(i + 1) * hr, :] = hb
        hbs.append(hb)
    lgs = [jnp.dot(hb, wr_ref[...], preferred_element_type=F32) + br_ref[...] for hb in hbs]
    lane = lax.broadcasted_iota(I32, lgs[0].shape, 1)
    for i in range(OPROJ_SPLIT):
        logits = jnp.where(lane < N_EXPERTS, lgs[i], NEG)
        mx = jnp.max(logits, axis=-1, keepdims=True)
        ex = jnp.exp(logits - mx)
        aff = ex / jnp.sum(ex, axis=-1, keepdims=True)
        at_ref[:, i * hr:(i + 1) * hr] = aff.T[:N_EXPERTS, :]


def _oproj(os_, ls_, dils, x, wo_bf16, gt, g, sh, sc, wr_pad, br_pad, seq, t):
    n, d = x.shape
    tpb = seq // t
    n_grp = len(os_)
    tok = pl.BlockSpec((t, d), lambda i: (i, 0))
    gspecs = [pl.BlockSpec((t // dil, o.shape[1]), lambda i: (i, 0)) for o, dil in zip(os_, dils)]
    vec = pl.BlockSpec((1, 1, d), lambda i: (i // tpb, 0, 0))
    row = pl.BlockSpec((1, d), lambda i: (0, 0))
    in_specs = gspecs + (gspecs if n_grp > 1 else []) + [
        tok, pl.BlockSpec(wo_bf16.shape, lambda i: (0, 0)), vec, row, vec, vec,
        pl.BlockSpec(wr_pad.shape, lambda i: (0, 0)), pl.BlockSpec((1, LANES), lambda i: (0, 0))]
    args = list(os_) + (list(ls_) if n_grp > 1 else []) + [
        x, wo_bf16, gt, g.reshape(1, d), sh, sc, wr_pad, br_pad]
    out_shape = [jax.ShapeDtypeStruct((n, d), F32), jax.ShapeDtypeStruct((n, d), BF16),
                 jax.ShapeDtypeStruct((N_EXPERTS, n), F32)]
    out_specs = [tok, tok, pl.BlockSpec((N_EXPERTS, t), lambda i: (0, i))]
    return pl.pallas_call(
        functools.partial(_oproj_kernel, dils=tuple(dils)),
        out_shape=out_shape, grid=(n // t,), in_specs=in_specs, out_specs=out_specs,
        scratch_shapes=[pltpu.VMEM((2 * n_grp, B_GW // LANES, t, LANES), F32)],
        compiler_params=_cparams(("arbitrary",)), name="oproj",
    )(*args)


def _thr_kernel(a_ref, tau_ref, need_ref, *, cap):
    bits = pltpu.bitcast(a_ref[...], I32)

    def body(it, tau):
        cand = tau | jnp.left_shift(jnp.int32(1), 30 - it)
        cnt = jnp.sum(jnp.where(bits >= cand[:, :1], 1.0, 0.0), axis=1, keepdims=True)
        return jnp.where(cnt >= cap, cand, tau)

    tau = lax.fori_loop(0, 31, body, jnp.zeros((N_EXPERTS, LANES), I32))
    cgt = jnp.sum(jnp.where(bits > tau[:, :1], 1.0, 0.0), axis=1, keepdims=True)
    tau_ref[...] = tau
    need_ref[...] = jnp.broadcast_to(cap - cgt, (N_EXPERTS, LANES))


def _thresholds(aff_t, cap):
    e, n = aff_t.shape
    vm = pl.BlockSpec(memory_space=pltpu.VMEM)
    return pl.pallas_call(
        functools.partial(_thr_kernel, cap=float(cap)),
        out_shape=[jax.ShapeDtypeStruct((e, LANES), I32), jax.ShapeDtypeStruct((e, LANES), F32)],
        in_specs=[vm], out_specs=[vm, vm],
        compiler_params=pltpu.CompilerParams(vmem_limit_bytes=VMEM_LIMIT), name="thr",
    )(aff_t)


def _sel_kernel(at_ref, tau_ref, need_ref, kp_ref, cnt_ref, run_ref):
    @pl.when(pl.program_id(0) == 0)
    def _():
        run_ref[...] = jnp.zeros_like(run_ref)

    c = MOE_CHUNK
    tau = tau_ref[...]
    need = need_ref[...]
    r_io = lax.broadcasted_iota(I32, (c, c), 0)
    c_io = lax.broadcasted_iota(I32, (c, c), 1)
    upper = jnp.where(r_io < c_io, 1.0, 0.0).astype(BF16)

    lane16 = lax.broadcasted_iota(I32, (N_EXPERTS, LANES), 1)

    gts, eqs, eq_ins, eq_cnts = [], [], [], []
    for ci in range(N_CHUNK):
        bits = pltpu.bitcast(at_ref[:, ci * c:(ci + 1) * c], I32)
        gts.append(bits > tau)
        eqs.append(bits == tau)
        eqf = jnp.where(eqs[ci], 1.0, 0.0)
        eq_ins.append(jnp.dot(eqf.astype(BF16), upper, preferred_element_type=F32))
        eq_cnts.append(jnp.sum(eqf, axis=1, keepdims=True))
    run = run_ref[...]
    sels, cums, cnts = [], [], []
    for ci in range(N_CHUNK):
        sels.append(gts[ci] | (eqs[ci] & (eq_ins[ci] + run < need)))
        self_ = jnp.where(sels[ci], 1.0, 0.0)
        cums.append(jnp.dot(self_.astype(BF16), upper, preferred_element_type=F32))
        cnts.append(jnp.sum(self_, axis=1, keepdims=True))
        run = run + eq_cnts[ci]
    run_ref[...] = run
    base = jnp.zeros((N_EXPERTS, LANES), F32)
    table = jnp.zeros((N_EXPERTS, LANES), F32)
    for ci in range(N_CHUNK):
        table = jnp.where(lane16 == ci, base, table)
        kp_ref[:, ci * c:(ci + 1) * c] = jnp.where(sels[ci], cums[ci] + base + 1.0, 0.0)
        base = base + cnts[ci]
    cnt_ref[0] = jnp.where(lane16 == N_CHUNK, base, table).astype(I32)


def _select(aff_t, tau, need):
    e, n = aff_t.shape
    nt = n // MOE_TILE
    full = pl.BlockSpec((e, LANES), lambda i: (0, 0))
    return pl.pallas_call(
        _sel_kernel,
        out_shape=[jax.ShapeDtypeStruct((e, n), F32),
                   jax.ShapeDtypeStruct((nt, e, LANES), I32)],
        grid=(nt,),
        in_specs=[pl.BlockSpec((e, MOE_TILE), lambda i: (0, i)), full, full],
        out_specs=[pl.BlockSpec((e, MOE_TILE), lambda i: (0, i)),
                   pl.BlockSpec((1, e, LANES), lambda i: (i, 0, 0))],
        scratch_shapes=[pltpu.VMEM((e, LANES), F32)],
        compiler_params=_cparams(("arbitrary",)), name="sel",
    )(aff_t, tau, need)


def _swiglu(x, wgu_ref, wd_ref):
    f = wd_ref.shape[1]
    hgu = jnp.dot(x, wgu_ref[0], preferred_element_type=F32)
    hg = hgu[:, :f]
    hu = hgu[:, f:]
    hid = (jax.nn.silu(hg) * hu).astype(BF16)
    return jnp.dot(hid, wd_ref[0], preferred_element_type=F32)


def _ffn_kernel(cb_ref, kp_ref, at_ref, h_ref, wgu_ref, wd_ref, o_ref,
                xy_ref, x_ref, yg_ref):
    t_id = pl.program_id(0)
    e_id = pl.program_id(1)

    def cb(ee, lane):
        return cb_ref[(t_id * N_EXPERTS + ee) * CB_STRIDE + lane]

    m = cb(e_id, N_CHUNK)
    m_max = cb(0, N_CHUNK)
    for ee in range(1, N_EXPERTS):
        m_max = jnp.maximum(m_max, cb(ee, N_CHUNK))
    fast = m_max <= MOE_CAP

    def gate_col(first_slot, rows):
        sub = lax.broadcasted_iota(I32, (rows, MOE_TILE), 0) + (first_slot + 1)
        hit = kp_ref[pl.ds(e_id, 1), :] == sub.astype(F32)
        return jnp.sum(jnp.where(hit, at_ref[pl.ds(e_id, 1), :], 0.0), axis=1, keepdims=True)

    cpw = MOE_WIN // MOE_CHUNK

    def win_rounds(w):
        nr = jnp.int32(0)
        for ee in range(N_EXPERTS):
            a = (cb(ee, cpw * w) >> 4) << 4
            nr = jnp.maximum(nr, (cb(ee, cpw * w + cpw) - a + MOE_SLOT - 1) // MOE_SLOT)
        return nr

    def slot_rows(w, r):
        return [jnp.minimum(((cb(ee, cpw * w) >> 4) << 4) + r * MOE_SLOT, MOE_CAP)
                for ee in range(N_EXPERTS)]

    def sel_matrix(w, rows):
        kpw = kp_ref[:, pl.ds(pl.multiple_of(w * MOE_WIN, MOE_WIN), MOE_WIN)]
        sub = lax.broadcasted_iota(I32, (MOE_SLOT, MOE_WIN), 0).astype(F32)
        parts = [jnp.where(kpw[ee:ee + 1, :] == sub + (rows[ee] + 1).astype(F32), 1.0, 0.0).astype(BF16)
                 for ee in range(N_EXPERTS)]
        return jnp.concatenate(parts, axis=0)

    def buf_off(ee, row):
        return pl.multiple_of(ee * MOE_CAP + row, 16)

    @pl.when(fast & (e_id == 0))
    def _dispatch():
        xy_ref[...] = jnp.zeros_like(xy_ref)

        def win(w, c1):
            hw = h_ref[pl.ds(pl.multiple_of(w * MOE_WIN, MOE_WIN), MOE_WIN), :]

            def rnd(r, c2):
                rows = slot_rows(w, r)
                x = jnp.dot(sel_matrix(w, rows), hw, preferred_element_type=F32)
                for ee in range(N_EXPERTS):
                    xy_ref[pl.ds(buf_off(ee, rows[ee]), MOE_SLOT), :] += (
                        x[ee * MOE_SLOT:(ee + 1) * MOE_SLOT, :].astype(BF16))
                return c2

            lax.fori_loop(0, win_rounds(w), rnd, 0)
            return c1

        lax.fori_loop(0, N_WIN, win, 0)

    def ffn_rows(first, rows):
        r0 = pl.multiple_of(e_id * MOE_CAP + first, 16)
        ye = _swiglu(xy_ref[pl.ds(r0, rows), :], wgu_ref, wd_ref)
        xy_ref[pl.ds(r0, rows), :] = (ye * gate_col(first, rows)).astype(BF16)

    @pl.when(fast & (m > 0))
    def _():
        ffn_rows(0, MOE_HEAD)

    @pl.when(fast & (m > MOE_HEAD))
    def _():
        ffn_rows(MOE_HEAD, MOE_CAP - MOE_HEAD)

    @pl.when(fast & (e_id == N_EXPERTS - 1))
    def _combine():
        def win(w, c1):
            w0 = pl.multiple_of(w * MOE_WIN, MOE_WIN)
            o_ref[pl.ds(w0, MOE_WIN), :] = jnp.zeros((MOE_WIN, o_ref.shape[1]), F32)

            def rnd(r, c2):
                rows = slot_rows(w, r)
                y = jnp.concatenate([xy_ref[pl.ds(buf_off(ee, rows[ee]), MOE_SLOT), :]
                                     for ee in range(N_EXPERTS)], axis=0)
                o_ref[pl.ds(w0, MOE_WIN), :] += lax.dot_general(
                    sel_matrix(w, rows), y, (((0,), (0,)), ((), ())), preferred_element_type=F32)
                return c2

            lax.fori_loop(0, win_rounds(w), rnd, 0)
            return c1

        lax.fori_loop(0, N_WIN, win, 0)

    @pl.when(jnp.logical_not(fast))
    def _general():
        @pl.when(e_id == 0)
        def _():
            o_ref[...] = jnp.zeros_like(o_ref)

        sub_sb = lax.broadcasted_iota(I32, (MOE_SB, MOE_TILE), 0).astype(F32)

        def superblock(si, carry):
            lo = si * MOE_SB
            kp_all = kp_ref[pl.ds(e_id, 1), :]
            p = jnp.where(kp_all == sub_sb + (lo + 1).astype(F32), 1.0, 0.0).astype(BF16)
            x_ref[...] = jnp.dot(p, h_ref[...], preferred_element_type=F32).astype(BF16)
            yg_ref[...] = jnp.zeros_like(yg_ref)

            def block(bi, c2):
                r0 = pl.multiple_of(bi * MOE_BLK, MOE_BLK)
                ye = _swiglu(x_ref[pl.ds(r0, MOE_BLK), :], wgu_ref, wd_ref)
                yg_ref[pl.ds(r0, MOE_BLK), :] = (ye * gate_col(lo + r0, MOE_BLK)).astype(BF16)
                return c2

            nblk = jnp.minimum((m - lo + MOE_BLK - 1) // MOE_BLK, MOE_SB // MOE_BLK)
            lax.fori_loop(0, nblk, block, 0)
            o_ref[...] += lax.dot_general(p, yg_ref[...], (((0,), (0,)), ((), ())),
                                          preferred_element_type=F32)
            return carry

        lax.fori_loop(0, (m + MOE_SB - 1) // MOE_SB, superblock, 0)


def _moe_ffn(h, kp, aff_t, cb, wgu, wd):
    n, d = h.shape
    nt = n // MOE_TILE
    f = wd.shape[1]
    return pl.pallas_call(
        _ffn_kernel,
        out_shape=jax.ShapeDtypeStruct((n, d), F32),
        grid=(nt, N_EXPERTS),
        in_specs=[pl.BlockSpec(memory_space=pltpu.SMEM),
                  pl.BlockSpec((N_EXPERTS, MOE_TILE), lambda t, e: (0, t)),
                  pl.BlockSpec((N_EXPERTS, MOE_TILE), lambda t, e: (0, t)),
                  pl.BlockSpec((MOE_TILE, d), lambda t, e: (t, 0)),
                  pl.BlockSpec((1, d, 2 * f), lambda t, e: (e, 0, 0)),
                  pl.BlockSpec((1, f, d), lambda t, e: (e, 0, 0))],
        out_specs=pl.BlockSpec((MOE_TILE, d), lambda t, e: (t, 0)),
        scratch_shapes=[pltpu.VMEM((N_EXPERTS * MOE_CAP + MOE_SLOT, d), BF16),
                        pltpu.VMEM((MOE_SB, d), BF16), pltpu.VMEM((MOE_SB, d), BF16)],
        compiler_params=_cparams(("arbitrary", "arbitrary")), name="ffn",
    )(cb, kp, aff_t, h, wgu, wd)


def _moe(h, aff_t, wgu, wd):
    n = h.shape[0]
    cap = max(1, EC_CAPACITY * n // N_EXPERTS)
    tau, need = _thresholds(aff_t, cap)
    kp, cb = _select(aff_t, tau, need)
    return _moe_ffn(h, kp, aff_t, cb[:, :, :CB_STRIDE].reshape(-1), wgu, wd)


def _final_kernel(x_ref, m_ref, gt_ref, g_ref, o_ref):
    x = x_ref[...] + gt_ref[0] * m_ref[...]
    ms = jnp.mean(x * x, axis=-1, keepdims=True)
    o_ref[...] = (x * lax.rsqrt(ms + EPS)) * g_ref[...]


def _final(x, moe, gt, g, seq):
    n, d = x.shape
    t = FINAL_TILE
    tpb = seq // t
    tok = pl.BlockSpec((t, d), lambda i: (i, 0))
    return pl.pallas_call(
        _final_kernel, out_shape=jax.ShapeDtypeStruct((n, d), F32), grid=(n // t,),
        in_specs=[tok, tok, pl.BlockSpec((1, 1, d), lambda i: (i // tpb, 0, 0)),
                  pl.BlockSpec((1, d), lambda i: (0, 0))],
        out_specs=tok, compiler_params=_cparams(("arbitrary",)), name="final",
    )(x, moe, gt, g.reshape(1, d))


def _trunk(x3, mods, p):
    bsz, seq, d = x3.shape
    n = bsz * seq
    x = x3.reshape(n, d)
    tables = _rope_tables(seq)
    scale = HEAD_DIM ** -0.5
    sh1, sc1, gt1, sh2, sc2, gt2 = [m[:, None, :] for m in jnp.split(mods[0], 6, axis=-1)]
    qw, kw = A_HEADS * HEAD_DIM, A_KV_HEADS * HEAD_DIM
    pieces = ((0, qw, True, scale, 1, False), (qw, kw, True, 1.0, 1, True),
              (qw + kw, kw, False, 1.0, 1, True))
    q, k, v = _qkv(x, None, p["g_mix"][0], sh1, sc1, p["w_qkv_a"], tables, seq, pieces, TOK_TILE_A)
    (o,) = _banded_attention(
        q.reshape(bsz, seq, qw), k.reshape(bsz, seq, 2 * kw), v.reshape(bsz, seq, 2 * kw),
        bsz=bsz, seq_n=seq, dil=1, q_w=qw, k_w=2 * kw, win=A_WINDOW,
        pairs_per_kgroup=A_HEADS // A_KV_HEADS // 2, sink=p["sink_a"], want_lse=False,
        out_dtype=BF16)
    x, h, aff_t = _oproj([o.reshape(n, qw)], None, (1,), x, p["w_o_a"], gt1, p["g_ffn"][0],
                         sh2, sc2, p["w_router"][0], p["b_router"][0], seq, TOK_TILE_A)
    moe = _moe(h, aff_t, p["w_gate_up"][0], p["w_down"][0])
    gt2_prev = gt2
    sh1, sc1, gt1, sh2, sc2, gt2 = [m[:, None, :] for m in jnp.split(mods[1], 6, axis=-1)]
    ng = len(B_GROUPS)
    dils = tuple(dil for _, dil in B_GROUPS)
    pieces = tuple((part * ng * B_GW + gi * B_GW, B_GW, part < 2, scale if part == 0 else 1.0, dils[gi], False)
                   for gi in range(ng) for part in range(3))
    res = _qkv(x, (moe, gt2_prev), p["g_mix"][1], sh1, sc1, p["w_qkv_b"], tables, seq, pieces, TOK_TILE)
    x = res[0]
    outs, lses = [], []
    for gi, (window, dil) in enumerate(B_GROUPS):
        sn = seq // dil
        qg, kg, vg = [r.reshape(bsz, sn, dil * B_GW) for r in res[1 + 3 * gi:4 + 3 * gi]]
        og, lg = _banded_attention(
            qg, kg, vg, bsz=bsz, seq_n=sn, dil=dil, q_w=B_GW, k_w=B_GW,
            win=(window // 2) // dil, pairs_per_kgroup=1, sink=None, want_lse=True,
            out_dtype=BF16)
        outs.append(og.reshape(bsz * sn, dil * B_GW))
        lses.append(lg.reshape(bsz * sn, dil * B_GW))
    x, h, aff_t = _oproj(outs, lses, dils, x, p["w_o_b"], gt1, p["g_ffn"][1],
                         sh2, sc2, p["w_router"][1], p["b_router"][1], seq, TOK_TILE)
    moe = _moe(h, aff_t, p["w_gate_up"][1], p["w_down"][1])
    y = _final(x, moe, gt2, p["g_final"], seq)
    return y.reshape(bsz, seq, d)


def kernel(x_prompt, x_sample, c_prompt, c_sample, w_ada, b_ada, g_mix, g_ffn, w_qkv_a, sink_a, w_o_a, w_qkv_b, w_o_b, w_router, b_router, w_gate, w_up, w_down, g_final):
    bp = c_prompt.shape[0]
    bs = c_sample.shape[0]
    rows = -(-(bp + bs) // SUBLANES) * SUBLANES
    c_all = jnp.zeros((rows, D_MODEL), F32).at[:bp].set(c_prompt).at[bp:bp + bs].set(c_sample)
    mods = _ada(c_all, w_ada, b_ada)
    depth = w_ada.shape[0]
    wr_pad = jnp.zeros((depth, D_MODEL, LANES), F32).at[:, :, :N_EXPERTS].set(w_router).astype(BF16)
    br_pad = jnp.zeros((depth, 1, LANES), F32).at[:, 0, :N_EXPERTS].set(b_router)
    p = dict(
        g_mix=g_mix, g_ffn=g_ffn, g_final=g_final, sink_a=sink_a[0],
        w_qkv_a=w_qkv_a[0].astype(BF16), w_o_a=w_o_a[0].astype(BF16),
        w_qkv_b=w_qkv_b[0].astype(BF16), w_o_b=w_o_b[0].astype(BF16),
        w_router=wr_pad, b_router=br_pad,
        w_gate_up=jnp.concatenate([w_gate, w_up], axis=-1).astype(BF16), w_down=w_down.astype(BF16),
    )
    y_prompt = _trunk(x_prompt, mods[:, :bp], p)
    y_sample = _trunk(x_sample, mods[:, bp:bp + bs], p)
    return (y_prompt, y_sample)
```

```python
import functools

import jax
import jax.numpy as jnp
from jax import lax
from jax.experimental import pallas as pl
from jax.experimental.pallas import tpu as pltpu

F32 = jnp.float32
BF16 = jnp.bfloat16
I32 = jnp.int32

D_MODEL = 1024
HEAD_DIM = 64
A_HEADS = 16
A_KV_HEADS = 4
A_WINDOW = 128
B_GROUPS = ((128, 1), (512, 4), (2048, 16))
B_HEADS = 8
B_GW = B_HEADS * HEAD_DIM
N_EXPERTS = 16
EXPERT_FF = 1024
EC_CAPACITY = 2
ROPE_THETA = 10000.0
EPS = 1e-6
NEG = -1e30

LANES = 128
SUBLANES = 8
VMEM_LIMIT = 56 * 1024 * 1024

TOK_TILE = 512
TOK_TILE_A = 1024
FINAL_TILE = 2048
OPROJ_SPLIT = 2
ATT_TQ = 512
ATT_QB = 128
ATT_AHEAD = 3
MOE_TILE = 2048
MOE_CHUNK = 128
MOE_SB = 384
MOE_CAP = 384
MOE_WIN = 256
MOE_SLOT = 64
N_WIN = MOE_TILE // MOE_WIN
CB_STRIDE = 32
MOE_BLK = 128
MOE_HEAD = 288
N_CHUNK = MOE_TILE // MOE_CHUNK


def _cparams(sem):
    return pltpu.CompilerParams(dimension_semantics=sem, vmem_limit_bytes=VMEM_LIMIT)


def _norm_mod(x, g, sh, sc):
    ms = jnp.mean(x * x, axis=-1, keepdims=True)
    y = (x * lax.rsqrt(ms + EPS)) * g
    return y * (1.0 + sc) + sh


def _ada_kernel(c_ref, w_ref, b_ref, o_ref):
    s = jax.nn.silu(c_ref[...])
    o_ref[0] = jnp.dot(s, w_ref[0], preferred_element_type=F32,
                       precision=lax.Precision.HIGHEST) + b_ref[0]


def _ada(c_all, w_ada, b_ada):
    depth, d, n6 = w_ada.shape
    rows = c_all.shape[0]
    tn = 1536
    return pl.pallas_call(
        _ada_kernel,
        out_shape=jax.ShapeDtypeStruct((depth, rows, n6), F32),
        grid=(depth, n6 // tn),
        in_specs=[
            pl.BlockSpec((rows, d), lambda l, j: (0, 0)),
            pl.BlockSpec((1, d, tn), lambda l, j: (l, 0, j)),
            pl.BlockSpec((1, 1, tn), lambda l, j: (l, 0, j)),
        ],
        out_specs=pl.BlockSpec((1, rows, tn), lambda l, j: (l, 0, j)),
        compiler_params=_cparams(("arbitrary", "arbitrary")),
        name="ada",
    )(c_all, w_ada, b_ada.reshape(depth, 1, n6))


def _rope_tables(seq):
    half = HEAD_DIM // 2
    inv = ROPE_THETA ** (-jnp.arange(half, dtype=F32) / half)
    ang = jnp.arange(seq).astype(F32)[:, None] * inv[None, :]
    cos = jnp.cos(ang)
    sin = jnp.sin(ang)
    zero = jnp.zeros_like(sin)
    c = jnp.concatenate([cos, cos, cos, cos], axis=1)
    a = jnp.concatenate([-sin, zero, -sin, zero], axis=1)
    b = jnp.concatenate([zero, sin, zero, sin], axis=1)
    return c, a, b


def _rope_group(xg, c, a, b):
    return xg * c + pltpu.roll(xg, LANES - HEAD_DIM // 2, 1) * a + pltpu.roll(xg, HEAD_DIM // 2, 1) * b


def _qkv_kernel(*refs, pieces, has_res):
    t = refs[0].shape[0]
    if has_res:
        x_ref, m_ref, gt_ref, g_ref, sh_ref, sc_ref, w_ref, c_ref, a_ref, b_ref, xo_ref = refs[:11]
        out_refs = refs[11:11 + len(pieces)]
        x = x_ref[...] + gt_ref[0] * m_ref[...]
        xo_ref[...] = x
    else:
        x_ref, g_ref, sh_ref, sc_ref, w_ref, c_ref, a_ref, b_ref = refs[:8]
        out_refs = refs[8:8 + len(pieces)]
        x = x_ref[...]
    s_ref = refs[-1]
    h = _norm_mod(x, g_ref[...], sh_ref[0], sc_ref[0]).astype(BF16)
    acc = jnp.dot(h, w_ref[...], preferred_element_type=F32)
    c = c_ref[...]
    a = a_ref[...]
    b = b_ref[...]
    slot = 0
    low_half = lax.broadcasted_iota(I32, (1, LANES), 1) < HEAD_DIM
    for (c0, w, rope, scale, dil, dup), o_ref in zip(pieces, out_refs):
        ng = w // LANES
        for j in range(ng):
            xg = acc[:, c0 + j * LANES:c0 + (j + 1) * LANES]
            if rope:
                xg = _rope_group(xg, c, a, b)
            if scale != 1.0:
                xg = xg * scale
            if dup:
                sw = pltpu.roll(xg, HEAD_DIM, 1)
                o_ref[:, 2 * j * LANES:(2 * j + 1) * LANES] = jnp.where(low_half, xg, sw).astype(BF16)
                o_ref[:, (2 * j + 1) * LANES:(2 * j + 2) * LANES] = jnp.where(low_half, sw, xg).astype(BF16)
            elif dil == 1:
                o_ref[:, j * LANES:(j + 1) * LANES] = xg.astype(BF16)
            else:
                s_ref[slot, j] = xg
        if dil > 1:
            rows = t // dil
            for r in range(dil):
                for j in range(ng):
                    o_ref[:, r * w + j * LANES:r * w + (j + 1) * LANES] = (
                        s_ref.at[slot, j][pl.ds(r, rows, stride=dil), :].astype(BF16))
            slot += 1


def _qkv(x, res, g, sh, sc, w_bf16, tables, seq, pieces, t):
    n, d = x.shape
    tpb = seq // t
    c, a, b = tables
    tok = pl.BlockSpec((t, d), lambda i: (i, 0))
    vec = pl.BlockSpec((1, 1, d), lambda i: (i // tpb, 0, 0))
    tab = pl.BlockSpec((t, LANES), lambda i: (i % tpb, 0))
    in_specs = [tok]
    args = [x]
    if res is not None:
        in_specs += [tok, vec]
        args += [res[0], res[1]]
    in_specs += [pl.BlockSpec((1, d), lambda i: (0, 0)), vec, vec,
                 pl.BlockSpec(w_bf16.shape, lambda i: (0, 0)), tab, tab, tab]
    args += [g.reshape(1, d), sh, sc, w_bf16, c, a, b]
    widths = [dil * w * (2 if dup else 1) for (_, w, _, _, dil, dup) in pieces]
    out_shape = [jax.ShapeDtypeStruct((n // p[4], wd), BF16) for p, wd in zip(pieces, widths)]
    out_specs = [pl.BlockSpec((t // p[4], wd), lambda i: (i, 0)) for p, wd in zip(pieces, widths)]
    if res is not None:
        out_shape = [jax.ShapeDtypeStruct((n, d), F32)] + out_shape
        out_specs = [tok] + out_specs
    n_strided = max(1, sum(1 for p in pieces if p[4] > 1))
    kern = functools.partial(_qkv_kernel, pieces=pieces, has_res=res is not None)
    return pl.pallas_call(
        kern, out_shape=out_shape, grid=(n // t,), in_specs=in_specs, out_specs=out_specs,
        scratch_shapes=[pltpu.VMEM((n_strided, B_GW // LANES, t, LANES), F32)],
        compiler_params=_cparams(("arbitrary",)), name="qkv",
    )(*args)


def _attn_kernel(*refs, tq, win, n_pair, pairs_per_kgroup, seq_len, has_sink, want_lse):
    pos = 0
    if has_sink:
        sink_ref = refs[0]
        pos = 1
    q_ref, kp_ref, kc_ref, kn_ref, vp_ref, vc_ref, vn_ref = refs[pos:pos + 7]
    o_ref = refs[pos + 7]
    lse_ref = refs[pos + 8] if want_lse else None
    i = pl.program_id(2)
    qb = ATT_QB
    kw = qb + 2 * win
    k = jnp.concatenate([kp_ref[0], kc_ref[0], kn_ref[0]], axis=0)
    v = jnp.concatenate([vp_ref[0], vc_ref[0], vn_ref[0]], axis=0)
    low_half = lax.broadcasted_iota(I32, (1, LANES), 1) < HEAD_DIM
    r_io = lax.broadcasted_iota(I32, (qb, kw), 0)
    c_io = lax.broadcasted_iota(I32, (qb, kw), 1)
    band = (c_io >= r_io) & (c_io - r_io <= 2 * win)
    biases = []
    for j in range(tq // qb):
        kpos = i * tq - win + j * qb + c_io
        biases.append(jnp.where(band & (kpos >= 0) & (kpos < seq_len), 0.0, NEG))
    masked = []
    for gk in range(n_pair // pairs_per_kgroup):
        kg = k[:, gk * LANES:(gk + 1) * LANES]
        vg = v[:, gk * LANES:(gk + 1) * LANES]
        zero = jnp.zeros_like(kg)
        masked.append(((jnp.where(low_half, kg, zero), jnp.where(low_half, vg, zero)),
                       (jnp.where(low_half, zero, kg), jnp.where(low_half, zero, vg))))
    units = [(p, j) for p in range(n_pair) for j in range(tq // qb)]

    def scores(u):
        p, j = u
        qp = q_ref[0, j * qb:(j + 1) * qb, p * LANES:(p + 1) * LANES]
        return [lax.dot_general(qp, kh[j * qb:j * qb + kw], (((1,), (1,)), ((), ())),
                                preferred_element_type=F32) + biases[j]
                for kh, _ in masked[p // pairs_per_kgroup]]

    def finish(u, ss):
        p, j = u
        o_pair = None
        lses = []
        for half, s in enumerate(ss):
            vh = masked[p // pairs_per_kgroup][half][1]
            m = jnp.max(s, axis=-1, keepdims=True)
            if has_sink:
                sk = sink_ref[2 * p + half]
                m = jnp.maximum(m, sk)
            e = jnp.exp(s - m)
            l = jnp.sum(e, axis=-1, keepdims=True)
            if has_sink:
                l = l + jnp.exp(sk - m)
            o_h = jnp.dot(e.astype(BF16), vh[j * qb:j * qb + kw], preferred_element_type=F32) / l
            o_pair = o_h if o_pair is None else o_pair + o_h
            lses.append(m + jnp.log(l))
        o_ref[0, j * qb:(j + 1) * qb, p * LANES:(p + 1) * LANES] = o_pair.astype(o_ref.dtype)
        if want_lse:
            lse_ref[0, j * qb:(j + 1) * qb, p * LANES:(p + 1) * LANES] = jnp.where(
                low_half, lses[0], lses[1])

    pending = [scores(u) for u in units[:ATT_AHEAD]]
    for idx, u in enumerate(units):
        if idx + ATT_AHEAD < len(units):
            pending.append(scores(units[idx + ATT_AHEAD]))
        finish(u, pending.pop(0))


def _banded_attention(q, k, v, *, bsz, seq_n, dil, q_w, k_w,
                      win, pairs_per_kgroup, sink, want_lse, out_dtype):
    tq = min(ATT_TQ, seq_n)
    nt = seq_n // tq
    wpt = tq // win
    nwb = seq_n // win

    def qmap(b, r, i):
        return (b, i, r)

    def kprev(b, r, i):
        return (b, jnp.maximum(i * wpt - 1, 0), r)

    def knext(b, r, i):
        return (b, jnp.minimum((i + 1) * wpt, nwb - 1), r)

    in_specs = []
    args = []
    if sink is not None:
        in_specs.append(pl.BlockSpec(memory_space=pltpu.SMEM))
        args.append(sink)
    in_specs += [pl.BlockSpec((1, tq, q_w), qmap),
                 pl.BlockSpec((1, win, k_w), kprev), pl.BlockSpec((1, tq, k_w), qmap),
                 pl.BlockSpec((1, win, k_w), knext),
                 pl.BlockSpec((1, win, k_w), kprev), pl.BlockSpec((1, tq, k_w), qmap),
                 pl.BlockSpec((1, win, k_w), knext)]
    args += [q, k, k, k, v, v, v]
    out_shape = [jax.ShapeDtypeStruct((bsz, seq_n, dil * q_w), out_dtype)]
    out_specs = [pl.BlockSpec((1, tq, q_w), qmap)]
    if want_lse:
        out_shape.append(jax.ShapeDtypeStruct((bsz, seq_n, dil * q_w), F32))
        out_specs.append(pl.BlockSpec((1, tq, q_w), qmap))
    kern = functools.partial(_attn_kernel, tq=tq, win=win, n_pair=q_w // LANES,
                             pairs_per_kgroup=pairs_per_kgroup, seq_len=seq_n,
                             has_sink=sink is not None, want_lse=want_lse)
    return pl.pallas_call(
        kern, out_shape=out_shape, grid=(bsz, dil, nt), in_specs=in_specs, out_specs=out_specs,
        compiler_params=_cparams(("arbitrary", "arbitrary", "arbitrary")), name="attn",
    )(*args)


def _oproj_kernel(*refs, dils):
    t = refs[0].shape[0] * dils[0]
    n_grp = len(dils)
    if n_grp == 1:
        o_refs, l_refs, rest = refs[:1], (), refs[1:]
    else:
        o_refs, l_refs, rest = refs[:n_grp], refs[n_grp:2 * n_grp], refs[2 * n_grp:]
    (x_ref, wo_ref, gt_ref, g_ref, sh_ref, sc_ref, wr_ref, br_ref,
     xo_ref, h_ref, at_ref, s_ref) = rest
    if n_grp == 1:
        o = o_refs[0][...]
    else:
        ng = B_GW // LANES

        def natural(ref, slot, dil):
            if dil == 1:
                return [ref[:, j * LANES:(j + 1) * LANES].astype(F32) for j in range(ng)]
            rows = t // dil
            for r in range(dil):
                for j in range(ng):
                    s_ref.at[slot, j][pl.ds(r, rows, stride=dil), :] = (
                        ref[:, r * B_GW + j * LANES:r * B_GW + (j + 1) * LANES].astype(F32))
            return [s_ref[slot, j] for j in range(ng)]

        os_ = [natural(o_refs[gi], 2 * gi, dils[gi]) for gi in range(n_grp)]
        ls_ = [natural(l_refs[gi], 2 * gi + 1, dils[gi]) for gi in range(n_grp)]
        cols = []
        for j in range(ng):
            ls = [ls_[gi][j] for gi in range(n_grp)]
            mx = ls[0]
            for v in ls[1:]:
                mx = jnp.maximum(mx, v)
            es = [jnp.exp(v - mx) for v in ls]
            den = es[0]
            for v in es[1:]:
                den = den + v
            oj = (es[0] / den) * os_[0][j]
            for gi in range(1, n_grp):
                oj = oj + (es[gi] / den) * os_[gi][j]
            cols.append(oj.astype(BF16))
        o = jnp.concatenate(cols, axis=1)
    hr = t // OPROJ_SPLIT
    ys = [jnp.dot(o[i * hr:(i + 1) * hr], wo_ref[...], preferred_element_type=F32)
          for i in range(OPROJ_SPLIT)]
    hbs = []
    for i in range(OPROJ_SPLIT):
        x = x_ref[i * hr:(i + 1) * hr, :] + gt_ref[0] * ys[i]
        xo_ref[i * hr:(i + 1) * hr, :] = x
        hb = _norm_mod(x, g_ref[...], sh_ref[0], sc_ref[0]).astype(BF16)
        h_ref[i * hr:(i + 1) * hr, :] = hb
        hbs.append(hb)
    lgs = [jnp.dot(hb, wr_ref[...], preferred_element_type=F32) + br_ref[...] for hb in hbs]
    lane = lax.broadcasted_iota(I32, lgs[0].shape, 1)
    for i in range(OPROJ_SPLIT):
        logits = jnp.where(lane < N_EXPERTS, lgs[i], NEG)
        mx = jnp.max(logits, axis=-1, keepdims=True)
        ex = jnp.exp(logits - mx)
        aff = ex / jnp.sum(ex, axis=-1, keepdims=True)
        at_ref[:, i * hr:(i + 1) * hr] = aff.T[:N_EXPERTS, :]


def _oproj(os_, ls_, dils, x, wo_bf16, gt, g, sh, sc, wr_pad, br_pad, seq, t):
    n, d = x.shape
    tpb = seq // t
    n_grp = len(os_)
    tok = pl.BlockSpec((t, d), lambda i: (i, 0))
    gspecs = [pl.BlockSpec((t // dil, o.shape[1]), lambda i: (i, 0)) for o, dil in zip(os_, dils)]
    vec = pl.BlockSpec((1, 1, d), lambda i: (i // tpb, 0, 0))
    row = pl.BlockSpec((1, d), lambda i: (0, 0))
    in_specs = gspecs + (gspecs if n_grp > 1 else []) + [
        tok, pl.BlockSpec(wo_bf16.shape, lambda i: (0, 0)), vec, row, vec, vec,
        pl.BlockSpec(wr_pad.shape, lambda i: (0, 0)), pl.BlockSpec((1, LANES), lambda i: (0, 0))]
    args = list(os_) + (list(ls_) if n_grp > 1 else []) + [
        x, wo_bf16, gt, g.reshape(1, d), sh, sc, wr_pad, br_pad]
    out_shape = [jax.ShapeDtypeStruct((n, d), F32), jax.ShapeDtypeStruct((n, d), BF16),
                 jax.ShapeDtypeStruct((N_EXPERTS, n), F32)]
    out_specs = [tok, tok, pl.BlockSpec((N_EXPERTS, t), lambda i: (0, i))]
    return pl.pallas_call(
        functools.partial(_oproj_kernel, dils=tuple(dils)),
        out_shape=out_shape, grid=(n // t,), in_specs=in_specs, out_specs=out_specs,
        scratch_shapes=[pltpu.VMEM((2 * n_grp, B_GW // LANES, t, LANES), F32)],
        compiler_params=_cparams(("arbitrary",)), name="oproj",
    )(*args)


def _thr_kernel(a_ref, tau_ref, need_ref, *, cap):
    bits = pltpu.bitcast(a_ref[...], I32)

    def body(it, tau):
        cand = tau | jnp.left_shift(jnp.int32(1), 30 - it)
        cnt = jnp.sum(jnp.where(bits >= cand[:, :1], 1.0, 0.0), axis=1, keepdims=True)
        return jnp.where(cnt >= cap, cand, tau)

    tau = lax.fori_loop(0, 31, body, jnp.zeros((N_EXPERTS, LANES), I32))
    cgt = jnp.sum(jnp.where(bits > tau[:, :1], 1.0, 0.0), axis=1, keepdims=True)
    tau_ref[...] = tau
    need_ref[...] = jnp.broadcast_to(cap - cgt, (N_EXPERTS, LANES))


def _thresholds(aff_t, cap):
    e, n = aff_t.shape
    vm = pl.BlockSpec(memory_space=pltpu.VMEM)
    return pl.pallas_call(
        functools.partial(_thr_kernel, cap=float(cap)),
        out_shape=[jax.ShapeDtypeStruct((e, LANES), I32), jax.ShapeDtypeStruct((e, LANES), F32)],
        in_specs=[vm], out_specs=[vm, vm],
        compiler_params=pltpu.CompilerParams(vmem_limit_bytes=VMEM_LIMIT), name="thr",
    )(aff_t)


def _sel_kernel(at_ref, tau_ref, need_ref, kp_ref, cnt_ref, run_ref):
    @pl.when(pl.program_id(0) == 0)
    def _():
        run_ref[...] = jnp.zeros_like(run_ref)

    c = MOE_CHUNK
    tau = tau_ref[...]
    need = need_ref[...]
    r_io = lax.broadcasted_iota(I32, (c, c), 0)
    c_io = lax.broadcasted_iota(I32, (c, c), 1)
    upper = jnp.where(r_io < c_io, 1.0, 0.0).astype(BF16)

    lane16 = lax.broadcasted_iota(I32, (N_EXPERTS, LANES), 1)

    gts, eqs, eq_ins, eq_cnts = [], [], [], []
    for ci in range(N_CHUNK):
        bits = pltpu.bitcast(at_ref[:, ci * c:(ci + 1) * c], I32)
        gts.append(bits > tau)
        eqs.append(bits == tau)
        eqf = jnp.where(eqs[ci], 1.0, 0.0)
        eq_ins.append(jnp.dot(eqf.astype(BF16), upper, preferred_element_type=F32))
        eq_cnts.append(jnp.sum(eqf, axis=1, keepdims=True))
    run = run_ref[...]
    sels, cums, cnts = [], [], []
    for ci in range(N_CHUNK):
        sels.append(gts[ci] | (eqs[ci] & (eq_ins[ci] + run < need)))
        self_ = jnp.where(sels[ci], 1.0, 0.0)
        cums.append(jnp.dot(self_.astype(BF16), upper, preferred_element_type=F32))
        cnts.append(jnp.sum(self_, axis=1, keepdims=True))
        run = run + eq_cnts[ci]
    run_ref[...] = run
    base = jnp.zeros((N_EXPERTS, LANES), F32)
    table = jnp.zeros((N_EXPERTS, LANES), F32)
    for ci in range(N_CHUNK):
        table = jnp.where(lane16 == ci, base, table)
        kp_ref[:, ci * c:(ci + 1) * c] = jnp.where(sels[ci], cums[ci] + base + 1.0, 0.0)
        base = base + cnts[ci]
    cnt_ref[0] = jnp.where(lane16 == N_CHUNK, base, table).astype(I32)


def _select(aff_t, tau, need):
    e, n = aff_t.shape
    nt = n // MOE_TILE
    full = pl.BlockSpec((e, LANES), lambda i: (0, 0))
    return pl.pallas_call(
        _sel_kernel,
        out_shape=[jax.ShapeDtypeStruct((e, n), F32),
                   jax.ShapeDtypeStruct((nt, e, LANES), I32)],
        grid=(nt,),
        in_specs=[pl.BlockSpec((e, MOE_TILE), lambda i: (0, i)), full, full],
        out_specs=[pl.BlockSpec((e, MOE_TILE), lambda i: (0, i)),
                   pl.BlockSpec((1, e, LANES), lambda i: (i, 0, 0))],
        scratch_shapes=[pltpu.VMEM((e, LANES), F32)],
        compiler_params=_cparams(("arbitrary",)), name="sel",
    )(aff_t, tau, need)


def _swiglu(x, wgu_ref, wd_ref):
    f = wd_ref.shape[1]
    parts = 2 if x.shape[0] >= 256 else 1
    pr = x.shape[0] // parts
    hgus = [jnp.dot(x[i * pr:(i + 1) * pr], wgu_ref[0], preferred_element_type=F32)
            for i in range(parts)]
    outs = []
    for hgu in hgus:
        hid = (jax.nn.silu(hgu[:, :f]) * hgu[:, f:]).astype(BF16)
        outs.append(jnp.dot(hid, wd_ref[0], preferred_element_type=F32))
    return outs[0] if parts == 1 else jnp.concatenate(outs, axis=0)


def _ffn_kernel(cb_ref, kp_ref, at_ref, h_ref, wgu_ref, wd_ref, o_ref,
                xy_ref, x_ref, yg_ref):
    t_id = pl.program_id(0)
    e_id = pl.program_id(1)

    def cb(ee, lane):
        return cb_ref[(t_id * N_EXPERTS + ee) * CB_STRIDE + lane]

    m = cb(e_id, N_CHUNK)
    m_max = cb(0, N_CHUNK)
    for ee in range(1, N_EXPERTS):
        m_max = jnp.maximum(m_max, cb(ee, N_CHUNK))
    fast = m_max <= MOE_CAP

    def gate_col(first_slot, rows):
        sub = lax.broadcasted_iota(I32, (rows, MOE_TILE), 0) + (first_slot + 1)
        hit = kp_ref[pl.ds(e_id, 1), :] == sub.astype(F32)
        return jnp.sum(jnp.where(hit, at_ref[pl.ds(e_id, 1), :], 0.0), axis=1, keepdims=True)

    cpw = MOE_WIN // MOE_CHUNK

    def win_rounds(w):
        nr = jnp.int32(0)
        for ee in range(N_EXPERTS):
            a = (cb(ee, cpw * w) >> 4) << 4
            nr = jnp.maximum(nr, (cb(ee, cpw * w + cpw) - a + MOE_SLOT - 1) // MOE_SLOT)
        return nr

    def slot_rows(w, r):
        return [jnp.minimum(((cb(ee, cpw * w) >> 4) << 4) + r * MOE_SLOT, MOE_CAP)
                for ee in range(N_EXPERTS)]

    def sel_matrix(w, rows):
        kpw = kp_ref[:, pl.ds(pl.multiple_of(w * MOE_WIN, MOE_WIN), MOE_WIN)]
        sub = lax.broadcasted_iota(I32, (MOE_SLOT, MOE_WIN), 0).astype(F32)
        parts = [jnp.where(kpw[ee:ee + 1, :] == sub + (rows[ee] + 1).astype(F32), 1.0, 0.0).astype(BF16)
                 for ee in range(N_EXPERTS)]
        return jnp.concatenate(parts, axis=0)

    def buf_off(ee, row):
        return pl.multiple_of(ee * MOE_CAP + row, 16)

    @pl.when(fast & (e_id == 0))
    def _dispatch():
        xy_ref[...] = jnp.zeros_like(xy_ref)

        def win(w, c1):
            hw = h_ref[pl.ds(pl.multiple_of(w * MOE_WIN, MOE_WIN), MOE_WIN), :]

            def rnd(r, c2):
                rows = slot_rows(w, r)
                x = jnp.dot(sel_matrix(w, rows), hw, preferred_element_type=F32)
                for ee in range(N_EXPERTS):
                    xy_ref[pl.ds(buf_off(ee, rows[ee]), MOE_SLOT), :] += (
                        x[ee * MOE_SLOT:(ee + 1) * MOE_SLOT, :].astype(BF16))
                return c2

            lax.fori_loop(0, win_rounds(w), rnd, 0)
            return c1

        lax.fori_loop(0, N_WIN, win, 0)

    def ffn_rows(first, rows):
        r0 = pl.multiple_of(e_id * MOE_CAP + first, 16)
        ye = _swiglu(xy_ref[pl.ds(r0, rows), :], wgu_ref, wd_ref)
        xy_ref[pl.ds(r0, rows), :] = (ye * gate_col(first, rows)).astype(BF16)

    @pl.when(fast & (m > 0))
    def _():
        ffn_rows(0, MOE_HEAD)

    @pl.when(fast & (m > MOE_HEAD))
    def _():
        ffn_rows(MOE_HEAD, MOE_CAP - MOE_HEAD)

    @pl.when(fast & (e_id == N_EXPERTS - 1))
    def _combine():
        def win(w, c1):
            w0 = pl.multiple_of(w * MOE_WIN, MOE_WIN)

            def contribution(r):
                rows = slot_rows(w, r)
                y = jnp.concatenate([xy_ref[pl.ds(buf_off(ee, rows[ee]), MOE_SLOT), :]
                                     for ee in range(N_EXPERTS)], axis=0)
                return lax.dot_general(sel_matrix(w, rows), y, (((0,), (0,)), ((), ())),
                                       preferred_element_type=F32)

            o_ref[pl.ds(w0, MOE_WIN), :] = contribution(0)

            def rnd(r, c2):
                o_ref[pl.ds(w0, MOE_WIN), :] += contribution(r)
                return c2

            lax.fori_loop(1, win_rounds(w), rnd, 0)
            return c1

        lax.fori_loop(0, N_WIN, win, 0)

    @pl.when(jnp.logical_not(fast))
    def _general():
        @pl.when(e_id == 0)
        def _():
            o_ref[...] = jnp.zeros_like(o_ref)

        sub_sb = lax.broadcasted_iota(I32, (MOE_SB, MOE_TILE), 0).astype(F32)

        def superblock(si, carry):
            lo = si * MOE_SB
            kp_all = kp_ref[pl.ds(e_id, 1), :]
            p = jnp.where(kp_all == sub_sb + (lo + 1).astype(F32), 1.0, 0.0).astype(BF16)
            x_ref[...] = jnp.dot(p, h_ref[...], preferred_element_type=F32).astype(BF16)
            yg_ref[...] = jnp.zeros_like(yg_ref)

            def block(bi, c2):
                r0 = pl.multiple_of(bi * MOE_BLK, MOE_BLK)
                ye = _swiglu(x_ref[pl.ds(r0, MOE_BLK), :], wgu_ref, wd_ref)
                yg_ref[pl.ds(r0, MOE_BLK), :] = (ye * gate_col(lo + r0, MOE_BLK)).astype(BF16)
                return c2

            nblk = jnp.minimum((m - lo + MOE_BLK - 1) // MOE_BLK, MOE_SB // MOE_BLK)
            lax.fori_loop(0, nblk, block, 0)
            o_ref[...] += lax.dot_general(p, yg_ref[...], (((0,), (0,)), ((), ())),
                                          preferred_element_type=F32)
            return carry

        lax.fori_loop(0, (m + MOE_SB - 1) // MOE_SB, superblock, 0)


def _moe_ffn(h, kp, aff_t, cb, wgu, wd):
    n, d = h.shape
    nt = n // MOE_TILE
    f = wd.shape[1]
    return pl.pallas_call(
        _ffn_kernel,
        out_shape=jax.ShapeDtypeStruct((n, d), F32),
        grid=(nt, N_EXPERTS),
        in_specs=[pl.BlockSpec(memory_space=pltpu.SMEM),
                  pl.BlockSpec((N_EXPERTS, MOE_TILE), lambda t, e: (0, t)),
                  pl.BlockSpec((N_EXPERTS, MOE_TILE), lambda t, e: (0, t)),
                  pl.BlockSpec((MOE_TILE, d), lambda t, e: (t, 0)),
                  pl.BlockSpec((1, d, 2 * f), lambda t, e: (e, 0, 0)),
                  pl.BlockSpec((1, f, d), lambda t, e: (e, 0, 0))],
        out_specs=pl.BlockSpec((MOE_TILE, d), lambda t, e: (t, 0)),
        scratch_shapes=[pltpu.VMEM((N_EXPERTS * MOE_CAP + MOE_SLOT, d), BF16),
                        pltpu.VMEM((MOE_SB, d), BF16), pltpu.VMEM((MOE_SB, d), BF16)],
        compiler_params=_cparams(("arbitrary", "arbitrary")), name="ffn",
    )(cb, kp, aff_t, h, wgu, wd)


def _moe(h, aff_t, wgu, wd):
    n = h.shape[0]
    cap = max(1, EC_CAPACITY * n // N_EXPERTS)
    tau, need = _thresholds(aff_t, cap)
    kp, cb = _select(aff_t, tau, need)
    return _moe_ffn(h, kp, aff_t, cb[:, :, :CB_STRIDE].reshape(-1), wgu, wd)


def _final_kernel(x_ref, m_ref, gt_ref, g_ref, o_ref):
    x = x_ref[...] + gt_ref[0] * m_ref[...]
    ms = jnp.mean(x * x, axis=-1, keepdims=True)
    o_ref[...] = (x * lax.rsqrt(ms + EPS)) * g_ref[...]


def _final(x, moe, gt, g, seq):
    n, d = x.shape
    t = FINAL_TILE
    tpb = seq // t
    tok = pl.BlockSpec((t, d), lambda i: (i, 0))
    return pl.pallas_call(
        _final_kernel, out_shape=jax.ShapeDtypeStruct((n, d), F32), grid=(n // t,),
        in_specs=[tok, tok, pl.BlockSpec((1, 1, d), lambda i: (i // tpb, 0, 0)),
                  pl.BlockSpec((1, d), lambda i: (0, 0))],
        out_specs=tok, compiler_params=_cparams(("arbitrary",)), name="final",
    )(x, moe, gt, g.reshape(1, d))


def _trunk(x3, mods, p):
    bsz, seq, d = x3.shape
    n = bsz * seq
    x = x3.reshape(n, d)
    tables = _rope_tables(seq)
    scale = HEAD_DIM ** -0.5
    sh1, sc1, gt1, sh2, sc2, gt2 = [m[:, None, :] for m in jnp.split(mods[0], 6, axis=-1)]
    qw, kw = A_HEADS * HEAD_DIM, A_KV_HEADS * HEAD_DIM
    pieces = ((0, qw, True, scale, 1, False), (qw, kw, True, 1.0, 1, True),
              (qw + kw, kw, False, 1.0, 1, True))
    q, k, v = _qkv(x, None, p["g_mix"][0], sh1, sc1, p["w_qkv_a"], tables, seq, pieces, TOK_TILE_A)
    (o,) = _banded_attention(
        q.reshape(bsz, seq, qw), k.reshape(bsz, seq, 2 * kw), v.reshape(bsz, seq, 2 * kw),
        bsz=bsz, seq_n=seq, dil=1, q_w=qw, k_w=2 * kw, win=A_WINDOW,
        pairs_per_kgroup=A_HEADS // A_KV_HEADS // 2, sink=p["sink_a"], want_lse=False,
        out_dtype=BF16)
    x, h, aff_t = _oproj([o.reshape(n, qw)], None, (1,), x, p["w_o_a"], gt1, p["g_ffn"][0],
                         sh2, sc2, p["w_router"][0], p["b_router"][0], seq, TOK_TILE_A)
    moe = _moe(h, aff_t, p["w_gate_up"][0], p["w_down"][0])
    gt2_prev = gt2
    sh1, sc1, gt1, sh2, sc2, gt2 = [m[:, None, :] for m in jnp.split(mods[1], 6, axis=-1)]
    ng = len(B_GROUPS)
    dils = tuple(dil for _, dil in B_GROUPS)
    pieces = tuple((part * ng * B_GW + gi * B_GW, B_GW, part < 2, scale if part == 0 else 1.0, dils[gi], False)
                   for gi in range(ng) for part in range(3))
    res = _qkv(x, (moe, gt2_prev), p["g_mix"][1], sh1, sc1, p["w_qkv_b"], tables, seq, pieces, TOK_TILE)
    x = res[0]
    outs, lses = [], []
    for gi, (window, dil) in enumerate(B_GROUPS):
        sn = seq // dil
        qg, kg, vg = [r.reshape(bsz, sn, dil * B_GW) for r in res[1 + 3 * gi:4 + 3 * gi]]
        og, lg = _banded_attention(
            qg, kg, vg, bsz=bsz, seq_n=sn, dil=dil, q_w=B_GW, k_w=B_GW,
            win=(window // 2) // dil, pairs_per_kgroup=1, sink=None, want_lse=True,
            out_dtype=BF16)
        outs.append(og.reshape(bsz * sn, dil * B_GW))
        lses.append(lg.reshape(bsz * sn, dil * B_GW))
    x, h, aff_t = _oproj(outs, lses, dils, x, p["w_o_b"], gt1, p["g_ffn"][1],
                         sh2, sc2, p["w_router"][1], p["b_router"][1], seq, TOK_TILE)
    moe = _moe(h, aff_t, p["w_gate_up"][1], p["w_down"][1])
    y = _final(x, moe, gt2, p["g_final"], seq)
    return y.reshape(bsz, seq, d)


def kernel(x_prompt, x_sample, c_prompt, c_sample, w_ada, b_ada, g_mix, g_ffn, w_qkv_a, sink_a, w_o_a, w_qkv_b, w_o_b, w_router, b_router, w_gate, w_up, w_down, g_final):
    bp = c_prompt.shape[0]
    bs = c_sample.shape[0]
    rows = -(-(bp + bs) // SUBLANES) * SUBLANES
    c_all = jnp.zeros((rows, D_MODEL), F32).at[:bp].set(c_prompt).at[bp:bp + bs].set(c_sample)
    mods = _ada(c_all, w_ada, b_ada)
    depth = w_ada.shape[0]
    wr_pad = jnp.zeros((depth, D_MODEL, LANES), F32).at[:, :, :N_EXPERTS].set(w_router).astype(BF16)
    br_pad = jnp.zeros((depth, 1, LANES), F32).at[:, 0, :N_EXPERTS].set(b_router)
    p = dict(
        g_mix=g_mix, g_ffn=g_ffn, g_final=g_final, sink_a=sink_a[0],
        w_qkv_a=w_qkv_a[0].astype(BF16), w_o_a=w_o_a[0].astype(BF16),
        w_qkv_b=w_qkv_b[0].astype(BF16), w_o_b=w_o_b[0].astype(BF16),
        w_router=wr_pad, b_router=br_pad,
        w_gate_up=jnp.concatenate([w_gate, w_up], axis=-1).astype(BF16), w_down=w_down.astype(BF16),
    )
    y_prompt = _trunk(x_prompt, mods[:, :bp], p)
    y_sample = _trunk(x_sample, mods[:, bp:bp + bs], p)
    return (y_prompt, y_sample)
```

```python
import functools

import jax
import jax.numpy as jnp
from jax import lax
from jax.experimental import pallas as pl
from jax.experimental.pallas import tpu as pltpu

F32 = jnp.float32
BF16 = jnp.bfloat16
I32 = jnp.int32

D_MODEL = 1024
HEAD_DIM = 64
A_HEADS = 16
A_KV_HEADS = 4
A_WINDOW = 128
B_GROUPS = ((128, 1), (512, 4), (2048, 16))
B_HEADS = 8
B_GW = B_HEADS * HEAD_DIM
N_EXPERTS = 16
EXPERT_FF = 1024
EC_CAPACITY = 2
ROPE_THETA = 10000.0
EPS = 1e-6
NEG = -1e30

LANES = 128
SUBLANES = 8
BF16_ROWS = 16
VMEM_LIMIT = 56 * 1024 * 1024

TOK_TILE = 512
TOK_TILE_A = 1024
FINAL_TILE = 2048
OPROJ_SPLIT = 2
ATT_TQ = 512
ATT_QB = 128
ATT_AHEAD = 3
MOE_TILE = 2048
MOE_CHUNK = 128
MOE_SB = 384
MOE_CAP = 384
MOE_WIN = 256
MOE_SLOT = 64
N_WIN = MOE_TILE // MOE_WIN
assert MOE_SLOT & (MOE_SLOT - 1) == 0 and MOE_SLOT % BF16_ROWS == 0 and MOE_CAP % MOE_SLOT == 0
CB_STRIDE = 32
MOE_BLK = 128
MOE_HEAD = 288
N_CHUNK = MOE_TILE // MOE_CHUNK


def _cparams(sem):
    return pltpu.CompilerParams(dimension_semantics=sem, vmem_limit_bytes=VMEM_LIMIT)


def _norm_mod(x, g, sh, sc):
    ms = jnp.mean(x * x, axis=-1, keepdims=True)
    y = (x * lax.rsqrt(ms + EPS)) * g
    return y * (1.0 + sc) + sh


def _ada_kernel(c_ref, w_ref, b_ref, o_ref):
    s = jax.nn.silu(c_ref[...])
    o_ref[0] = jnp.dot(s, w_ref[0], preferred_element_type=F32,
                       precision=lax.Precision.HIGHEST) + b_ref[0]


def _ada(c_all, w_ada, b_ada):
    depth, d, n6 = w_ada.shape
    rows = c_all.shape[0]
    tn = 1536
    return pl.pallas_call(
        _ada_kernel,
        out_shape=jax.ShapeDtypeStruct((depth, rows, n6), F32),
        grid=(depth, n6 // tn),
        in_specs=[
            pl.BlockSpec((rows, d), lambda l, j: (0, 0)),
            pl.BlockSpec((1, d, tn), lambda l, j: (l, 0, j)),
            pl.BlockSpec((1, 1, tn), lambda l, j: (l, 0, j)),
        ],
        out_specs=pl.BlockSpec((1, rows, tn), lambda l, j: (l, 0, j)),
        compiler_params=_cparams(("arbitrary", "arbitrary")),
        name="ada",
    )(c_all, w_ada, b_ada.reshape(depth, 1, n6))


def _rope_tables(seq):
    half = HEAD_DIM // 2
    inv = ROPE_THETA ** (-jnp.arange(half, dtype=F32) / half)
    ang = jnp.arange(seq).astype(F32)[:, None] * inv[None, :]
    cos = jnp.cos(ang)
    sin = jnp.sin(ang)
    zero = jnp.zeros_like(sin)
    c = jnp.concatenate([cos, cos, cos, cos], axis=1)
    a = jnp.concatenate([-sin, zero, -sin, zero], axis=1)
    b = jnp.concatenate([zero, sin, zero, sin], axis=1)
    return c, a, b


def _rope_group(xg, c, a, b):
    return xg * c + pltpu.roll(xg, LANES - HEAD_DIM // 2, 1) * a + pltpu.roll(xg, HEAD_DIM // 2, 1) * b


def _qkv_kernel(*refs, pieces, has_res):
    t = refs[0].shape[0]
    if has_res:
        x_ref, m_ref, gt_ref, g_ref, sh_ref, sc_ref, w_ref, c_ref, a_ref, b_ref, xo_ref = refs[:11]
        out_refs = refs[11:11 + len(pieces)]
        x = x_ref[...] + gt_ref[0] * m_ref[...]
        xo_ref[...] = x
    else:
        x_ref, g_ref, sh_ref, sc_ref, w_ref, c_ref, a_ref, b_ref = refs[:8]
        out_refs = refs[8:8 + len(pieces)]
        x = x_ref[...]
    s_ref = refs[-1]
    h = _norm_mod(x, g_ref[...], sh_ref[0], sc_ref[0]).astype(BF16)
    acc = jnp.dot(h, w_ref[...], preferred_element_type=F32)
    c = c_ref[...]
    a = a_ref[...]
    b = b_ref[...]
    slot = 0
    low_half = lax.broadcasted_iota(I32, (1, LANES), 1) < HEAD_DIM
    for (c0, w, rope, scale, dil, dup), o_ref in zip(pieces, out_refs):
        ng = w // LANES
        for j in range(ng):
            xg = acc[:, c0 + j * LANES:c0 + (j + 1) * LANES]
            if rope:
                xg = _rope_group(xg, c, a, b)
            if scale != 1.0:
                xg = xg * scale
            if dup:
                sw = pltpu.roll(xg, HEAD_DIM, 1)
                o_ref[:, 2 * j * LANES:(2 * j + 1) * LANES] = jnp.where(low_half, xg, sw).astype(BF16)
                o_ref[:, (2 * j + 1) * LANES:(2 * j + 2) * LANES] = jnp.where(low_half, sw, xg).astype(BF16)
            elif dil == 1:
                o_ref[:, j * LANES:(j + 1) * LANES] = xg.astype(BF16)
            else:
                s_ref[slot, j] = xg
        if dil > 1:
            rows = t // dil
            for r in range(dil):
                for j in range(ng):
                    o_ref[:, r * w + j * LANES:r * w + (j + 1) * LANES] = (
                        s_ref.at[slot, j][pl.ds(r, rows, stride=dil), :].astype(BF16))
            slot += 1


def _qkv(x, res, g, sh, sc, w_bf16, tables, seq, pieces, t):
    n, d = x.shape
    tpb = seq // t
    c, a, b = tables
    tok = pl.BlockSpec((t, d), lambda i: (i, 0))
    vec = pl.BlockSpec((1, 1, d), lambda i: (i // tpb, 0, 0))
    tab = pl.BlockSpec((t, LANES), lambda i: (i % tpb, 0))
    in_specs = [tok]
    args = [x]
    if res is not None:
        in_specs += [tok, vec]
        args += [res[0], res[1]]
    in_specs += [pl.BlockSpec((1, d), lambda i: (0, 0)), vec, vec,
                 pl.BlockSpec(w_bf16.shape, lambda i: (0, 0)), tab, tab, tab]
    args += [g.reshape(1, d), sh, sc, w_bf16, c, a, b]
    widths = [dil * w * (2 if dup else 1) for (_, w, _, _, dil, dup) in pieces]
    out_shape = [jax.ShapeDtypeStruct((n // p[4], wd), BF16) for p, wd in zip(pieces, widths)]
    out_specs = [pl.BlockSpec((t // p[4], wd), lambda i: (i, 0)) for p, wd in zip(pieces, widths)]
    if res is not None:
        out_shape = [jax.ShapeDtypeStruct((n, d), F32)] + out_shape
        out_specs = [tok] + out_specs
    n_strided = max(1, sum(1 for p in pieces if p[4] > 1))
    kern = functools.partial(_qkv_kernel, pieces=pieces, has_res=res is not None)
    return pl.pallas_call(
        kern, out_shape=out_shape, grid=(n // t,), in_specs=in_specs, out_specs=out_specs,
        scratch_shapes=[pltpu.VMEM((n_strided, B_GW // LANES, t, LANES), F32)],
        compiler_params=_cparams(("arbitrary",)), name="qkv",
    )(*args)


def _attn_kernel(*refs, tq, win, n_pair, pairs_per_kgroup, seq_len, has_sink, want_lse):
    pos = 0
    if has_sink:
        sink_ref = refs[0]
        pos = 1
    q_ref, kp_ref, kc_ref, kn_ref, vp_ref, vc_ref, vn_ref = refs[pos:pos + 7]
    o_ref = refs[pos + 7]
    lse_ref = refs[pos + 8] if want_lse else None
    i = pl.program_id(2)
    qb = ATT_QB
    kw = qb + 2 * win
    k = jnp.concatenate([kp_ref[0], kc_ref[0], kn_ref[0]], axis=0)
    v = jnp.concatenate([vp_ref[0], vc_ref[0], vn_ref[0]], axis=0)
    low_half = lax.broadcasted_iota(I32, (1, LANES), 1) < HEAD_DIM
    r_io = lax.broadcasted_iota(I32, (qb, kw), 0)
    c_io = lax.broadcasted_iota(I32, (qb, kw), 1)
    band = (c_io >= r_io) & (c_io - r_io <= 2 * win)
    biases = []
    for j in range(tq // qb):
        kpos = i * tq - win + j * qb + c_io
        biases.append(jnp.where(band & (kpos >= 0) & (kpos < seq_len), 0.0, NEG))
    masked = []
    for gk in range(n_pair // pairs_per_kgroup):
        kg = k[:, gk * LANES:(gk + 1) * LANES]
        vg = v[:, gk * LANES:(gk + 1) * LANES]
        zero = jnp.zeros_like(kg)
        masked.append(((jnp.where(low_half, kg, zero), jnp.where(low_half, vg, zero)),
                       (jnp.where(low_half, zero, kg), jnp.where(low_half, zero, vg))))
    units = [(p, j) for p in range(n_pair) for j in range(tq // qb)]

    def scores(u):
        p, j = u
        qp = q_ref[0, j * qb:(j + 1) * qb, p * LANES:(p + 1) * LANES]
        return [lax.dot_general(qp, kh[j * qb:j * qb + kw], (((1,), (1,)), ((), ())),
                                preferred_element_type=F32) + biases[j]
                for kh, _ in masked[p // pairs_per_kgroup]]

    def finish(u, ss):
        p, j = u
        o_pair = None
        lses = []
        for half, s in enumerate(ss):
            vh = masked[p // pairs_per_kgroup][half][1]
            m = jnp.max(s, axis=-1, keepdims=True)
            if has_sink:
                sk = sink_ref[2 * p + half]
                m = jnp.maximum(m, sk)
            e = jnp.exp(s - m)
            l = jnp.sum(e, axis=-1, keepdims=True)
            if has_sink:
                l = l + jnp.exp(sk - m)
            o_h = jnp.dot(e.astype(BF16), vh[j * qb:j * qb + kw], preferred_element_type=F32) / l
            o_pair = o_h if o_pair is None else o_pair + o_h
            lses.append(m + jnp.log(l))
        o_ref[0, j * qb:(j + 1) * qb, p * LANES:(p + 1) * LANES] = o_pair.astype(o_ref.dtype)
        if want_lse:
            lse_ref[0, j * qb:(j + 1) * qb, p * LANES:(p + 1) * LANES] = jnp.where(
                low_half, lses[0], lses[1])

    pending = [scores(u) for u in units[:ATT_AHEAD]]
    for idx, u in enumerate(units):
        if idx + ATT_AHEAD < len(units):
            pending.append(scores(units[idx + ATT_AHEAD]))
        finish(u, pending.pop(0))


def _banded_attention(q, k, v, *, bsz, seq_n, dil, q_w, k_w,
                      win, pairs_per_kgroup, sink, want_lse, out_dtype):
    tq = min(ATT_TQ, seq_n)
    nt = seq_n // tq
    wpt = tq // win
    nwb = seq_n // win

    def qmap(b, r, i):
        return (b, i, r)

    def kprev(b, r, i):
        return (b, jnp.maximum(i * wpt - 1, 0), r)

    def knext(b, r, i):
        return (b, jnp.minimum((i + 1) * wpt, nwb - 1), r)

    in_specs = []
    args = []
    if sink is not None:
        in_specs.append(pl.BlockSpec(memory_space=pltpu.SMEM))
        args.append(sink)
    in_specs += [pl.BlockSpec((1, tq, q_w), qmap),
                 pl.BlockSpec((1, win, k_w), kprev), pl.BlockSpec((1, tq, k_w), qmap),
                 pl.BlockSpec((1, win, k_w), knext),
                 pl.BlockSpec((1, win, k_w), kprev), pl.BlockSpec((1, tq, k_w), qmap),
                 pl.BlockSpec((1, win, k_w), knext)]
    args += [q, k, k, k, v, v, v]
    out_shape = [jax.ShapeDtypeStruct((bsz, seq_n, dil * q_w), out_dtype)]
    out_specs = [pl.BlockSpec((1, tq, q_w), qmap)]
    if want_lse:
        out_shape.append(jax.ShapeDtypeStruct((bsz, seq_n, dil * q_w), F32))
        out_specs.append(pl.BlockSpec((1, tq, q_w), qmap))
    kern = functools.partial(_attn_kernel, tq=tq, win=win, n_pair=q_w // LANES,
                             pairs_per_kgroup=pairs_per_kgroup, seq_len=seq_n,
                             has_sink=sink is not None, want_lse=want_lse)
    return pl.pallas_call(
        kern, out_shape=out_shape, grid=(bsz, dil, nt), in_specs=in_specs, out_specs=out_specs,
        compiler_params=_cparams(("arbitrary", "arbitrary", "arbitrary")), name="attn",
    )(*args)


def _oproj_kernel(*refs, dils):
    t = refs[0].shape[0] * dils[0]
    n_grp = len(dils)
    if n_grp == 1:
        o_refs, l_refs, rest = refs[:1], (), refs[1:]
    else:
        o_refs, l_refs, rest = refs[:n_grp], refs[n_grp:2 * n_grp], refs[2 * n_grp:]
    (x_ref, wo_ref, gt_ref, g_ref, sh_ref, sc_ref, wr_ref, br_ref,
     xo_ref, h_ref, at_ref, s_ref) = rest
    if n_grp == 1:
        o = o_refs[0][...]
    else:
        ng = B_GW // LANES

        def natural(ref, slot, dil):
            if dil == 1:
                return [ref[:, j * LANES:(j + 1) * LANES].astype(F32) for j in range(ng)]
            rows = t // dil
            for r in range(dil):
                for j in range(ng):
                    s_ref.at[slot, j][pl.ds(r, rows, stride=dil), :] = (
                        ref[:, r * B_GW + j * LANES:r * B_GW + (j + 1) * LANES].astype(F32))
            return [s_ref[slot, j] for j in range(ng)]

        os_ = [natural(o_refs[gi], 2 * gi, dils[gi]) for gi in range(n_grp)]
        ls_ = [natural(l_refs[gi], 2 * gi + 1, dils[gi]) for gi in range(n_grp)]
        cols = []
        for j in range(ng):
            ls = [ls_[gi][j] for gi in range(n_grp)]
            mx = ls[0]
            for v in ls[1:]:
                mx = jnp.maximum(mx, v)
            es = [jnp.exp(v - mx) for v in ls]
            den = es[0]
            for v in es[1:]:
                den = den + v
            oj = (es[0] / den) * os_[0][j]
            for gi in range(1, n_grp):
                oj = oj + (es[gi] / den) * os_[gi][j]
            cols.append(oj.astype(BF16))
        o = jnp.concatenate(cols, axis=1)
    hr = t // OPROJ_SPLIT
    ys = [jnp.dot(o[i * hr:(i + 1) * hr], wo_ref[...], preferred_element_type=F32)
          for i in range(OPROJ_SPLIT)]
    hbs = []
    for i in range(OPROJ_SPLIT):
        x = x_ref[i * hr:(i + 1) * hr, :] + gt_ref[0] * ys[i]
        xo_ref[i * hr:(i + 1) * hr, :] = x
        hb = _norm_mod(x, g_ref[...], sh_ref[0], sc_ref[0]).astype(BF16)
        h_ref[i * hr:(i + 1) * hr, :] = hb
        hbs.append(hb)
    lgs = [jnp.dot(hb, wr_ref[...], preferred_element_type=F32) + br_ref[...] for hb in hbs]
    lane = lax.broadcasted_iota(I32, lgs[0].shape, 1)
    for i in range(OPROJ_SPLIT):
        logits = jnp.where(lane < N_EXPERTS, lgs[i], NEG)
        mx = jnp.max(logits, axis=-1, keepdims=True)
        ex = jnp.exp(logits - mx)
        aff = ex / jnp.sum(ex, axis=-1, keepdims=True)
        at_ref[:, i * hr:(i + 1) * hr] = aff.T[:N_EXPERTS, :]


def _oproj(os_, ls_, dils, x, wo_bf16, gt, g, sh, sc, wr_pad, br_pad, seq, t):
    n, d = x.shape
    tpb = seq // t
    n_grp = len(os_)
    tok = pl.BlockSpec((t, d), lambda i: (i, 0))
    gspecs = [pl.BlockSpec((t // dil, o.shape[1]), lambda i: (i, 0)) for o, dil in zip(os_, dils)]
    vec = pl.BlockSpec((1, 1, d), lambda i: (i // tpb, 0, 0))
    row = pl.BlockSpec((1, d), lambda i: (0, 0))
    in_specs = gspecs + (gspecs if n_grp > 1 else []) + [
        tok, pl.BlockSpec(wo_bf16.shape, lambda i: (0, 0)), vec, row, vec, vec,
        pl.BlockSpec(wr_pad.shape, lambda i: (0, 0)), pl.BlockSpec((1, LANES), lambda i: (0, 0))]
    args = list(os_) + (list(ls_) if n_grp > 1 else []) + [
        x, wo_bf16, gt, g.reshape(1, d), sh, sc, wr_pad, br_pad]
    out_shape = [jax.ShapeDtypeStruct((n, d), F32), jax.ShapeDtypeStruct((n, d), BF16),
                 jax.ShapeDtypeStruct((N_EXPERTS, n), F32)]
    out_specs = [tok, tok, pl.BlockSpec((N_EXPERTS, t), lambda i: (0, i))]
    return pl.pallas_call(
        functools.partial(_oproj_kernel, dils=tuple(dils)),
        out_shape=out_shape, grid=(n // t,), in_specs=in_specs, out_specs=out_specs,
        scratch_shapes=[pltpu.VMEM((2 * n_grp, B_GW // LANES, t, LANES), F32)],
        compiler_params=_cparams(("arbitrary",)), name="oproj",
    )(*args)


def _thr_kernel(a_ref, tau_ref, need_ref, *, cap):
    bits = pltpu.bitcast(a_ref[...], I32)

    def body(it, tau):
        cand = tau | jnp.left_shift(jnp.int32(1), 30 - it)
        cnt = jnp.sum(jnp.where(bits >= cand[:, :1], 1.0, 0.0), axis=1, keepdims=True)
        return jnp.where(cnt >= cap, cand, tau)

    tau = lax.fori_loop(0, 31, body, jnp.zeros((N_EXPERTS, LANES), I32))
    cgt = jnp.sum(jnp.where(bits > tau[:, :1], 1.0, 0.0), axis=1, keepdims=True)
    tau_ref[...] = tau
    need_ref[...] = jnp.broadcast_to(cap - cgt, (N_EXPERTS, LANES))


def _thresholds(aff_t, cap):
    e, n = aff_t.shape
    vm = pl.BlockSpec(memory_space=pltpu.VMEM)
    return pl.pallas_call(
        functools.partial(_thr_kernel, cap=float(cap)),
        out_shape=[jax.ShapeDtypeStruct((e, LANES), I32), jax.ShapeDtypeStruct((e, LANES), F32)],
        in_specs=[vm], out_specs=[vm, vm],
        compiler_params=pltpu.CompilerParams(vmem_limit_bytes=VMEM_LIMIT), name="thr",
    )(aff_t)


def _sel_kernel(at_ref, tau_ref, need_ref, kp_ref, cnt_ref, run_ref):
    @pl.when(pl.program_id(0) == 0)
    def _():
        run_ref[...] = jnp.zeros_like(run_ref)

    c = MOE_CHUNK
    tau = tau_ref[...]
    need = need_ref[...]
    r_io = lax.broadcasted_iota(I32, (c, c), 0)
    c_io = lax.broadcasted_iota(I32, (c, c), 1)
    upper = jnp.where(r_io < c_io, 1.0, 0.0).astype(BF16)

    lane16 = lax.broadcasted_iota(I32, (N_EXPERTS, LANES), 1)

    gts, eqs, eq_ins, eq_cnts = [], [], [], []
    for ci in range(N_CHUNK):
        bits = pltpu.bitcast(at_ref[:, ci * c:(ci + 1) * c], I32)
        gts.append(bits > tau)
        eqs.append(bits == tau)
        eqf = jnp.where(eqs[ci], 1.0, 0.0)
        eq_ins.append(jnp.dot(eqf.astype(BF16), upper, preferred_element_type=F32))
        eq_cnts.append(jnp.sum(eqf, axis=1, keepdims=True))
    run = run_ref[...]
    sels, cums, cnts = [], [], []
    for ci in range(N_CHUNK):
        sels.append(gts[ci] | (eqs[ci] & (eq_ins[ci] + run < need)))
        self_ = jnp.where(sels[ci], 1.0, 0.0)
        cums.append(jnp.dot(self_.astype(BF16), upper, preferred_element_type=F32))
        cnts.append(jnp.sum(self_, axis=1, keepdims=True))
        run = run + eq_cnts[ci]
    run_ref[...] = run
    base = jnp.zeros((N_EXPERTS, LANES), F32)
    table = jnp.zeros((N_EXPERTS, LANES), F32)
    for ci in range(N_CHUNK):
        table = jnp.where(lane16 == ci, base, table)
        kp_ref[:, ci * c:(ci + 1) * c] = jnp.where(sels[ci], cums[ci] + base + 1.0, 0.0)
        base = base + cnts[ci]
    cnt_ref[0] = jnp.where(lane16 == N_CHUNK, base, table).astype(I32)


def _select(aff_t, tau, need):
    e, n = aff_t.shape
    nt = n // MOE_TILE
    full = pl.BlockSpec((e, LANES), lambda i: (0, 0))
    return pl.pallas_call(
        _sel_kernel,
        out_shape=[jax.ShapeDtypeStruct((e, n), F32),
                   jax.ShapeDtypeStruct((nt, e, LANES), I32)],
        grid=(nt,),
        in_specs=[pl.BlockSpec((e, MOE_TILE), lambda i: (0, i)), full, full],
        out_specs=[pl.BlockSpec((e, MOE_TILE), lambda i: (0, i)),
                   pl.BlockSpec((1, e, LANES), lambda i: (i, 0, 0))],
        scratch_shapes=[pltpu.VMEM((e, LANES), F32)],
        compiler_params=_cparams(("arbitrary",)), name="sel",
    )(aff_t, tau, need)


def _swiglu(x, wgu_ref, wd_ref):
    f = wd_ref.shape[1]
    parts = 2 if x.shape[0] >= 256 else 1
    pr = x.shape[0] // parts
    hgus = [jnp.dot(x[i * pr:(i + 1) * pr], wgu_ref[0], preferred_element_type=F32)
            for i in range(parts)]
    outs = []
    for hgu in hgus:
        hid = (jax.nn.silu(hgu[:, :f]) * hgu[:, f:]).astype(BF16)
        outs.append(jnp.dot(hid, wd_ref[0], preferred_element_type=F32))
    return outs[0] if parts == 1 else jnp.concatenate(outs, axis=0)


def _ffn_kernel(cb_ref, kp_ref, at_ref, h_ref, wgu_ref, wd_ref, o_ref,
                xy_ref, x_ref, yg_ref):
    t_id = pl.program_id(0)
    e_id = pl.program_id(1)

    def cb(ee, lane):
        return cb_ref[(t_id * N_EXPERTS + ee) * CB_STRIDE + lane]

    m = cb(e_id, N_CHUNK)
    m_max = cb(0, N_CHUNK)
    for ee in range(1, N_EXPERTS):
        m_max = jnp.maximum(m_max, cb(ee, N_CHUNK))
    fast = m_max <= MOE_CAP

    def gate_col(first_slot, rows):
        sub = lax.broadcasted_iota(I32, (rows, MOE_TILE), 0) + (first_slot + 1)
        hit = kp_ref[pl.ds(e_id, 1), :] == sub.astype(F32)
        return jnp.sum(jnp.where(hit, at_ref[pl.ds(e_id, 1), :], 0.0), axis=1, keepdims=True)

    cpw = MOE_WIN // MOE_CHUNK

    def slot_start(ee, w):
        return cb(ee, cpw * w) & -BF16_ROWS

    def win_rounds(w):
        span = jnp.int32(0)
        for ee in range(N_EXPERTS):
            span = jnp.maximum(span, cb(ee, cpw * w + cpw) - slot_start(ee, w))
        return lax.shift_right_logical(span + (MOE_SLOT - 1), jnp.int32(MOE_SLOT.bit_length() - 1))

    def slot_rows(w, r):
        return [jnp.minimum(slot_start(ee, w) + r * MOE_SLOT, MOE_CAP) for ee in range(N_EXPERTS)]

    def sel_matrix(w, rows):
        kpw = kp_ref[:, pl.ds(pl.multiple_of(w * MOE_WIN, MOE_WIN), MOE_WIN)]
        sub = lax.broadcasted_iota(I32, (MOE_SLOT, MOE_WIN), 0).astype(F32)
        parts = [jnp.where(kpw[ee:ee + 1, :] == sub + (rows[ee] + 1).astype(F32), 1.0, 0.0).astype(BF16)
                 for ee in range(N_EXPERTS)]
        return jnp.concatenate(parts, axis=0)

    def buf_off(ee, row):
        return pl.multiple_of(ee * MOE_CAP + row, BF16_ROWS)

    @pl.when(fast & (e_id == 0))
    def _dispatch():
        xy_ref[...] = jnp.zeros_like(xy_ref)

        def win(w, c1):
            hw = h_ref[pl.ds(pl.multiple_of(w * MOE_WIN, MOE_WIN), MOE_WIN), :]

            def rnd(r, c2):
                rows = slot_rows(w, r)
                x = jnp.dot(sel_matrix(w, rows), hw, preferred_element_type=F32)
                for ee in range(N_EXPERTS):
                    xy_ref[pl.ds(buf_off(ee, rows[ee]), MOE_SLOT), :] += (
                        x[ee * MOE_SLOT:(ee + 1) * MOE_SLOT, :].astype(BF16))
                return c2

            lax.fori_loop(0, win_rounds(w), rnd, 0)
            return c1

        lax.fori_loop(0, N_WIN, win, 0)

    def ffn_rows(first, rows):
        r0 = pl.multiple_of(e_id * MOE_CAP + first, BF16_ROWS)
        ye = _swiglu(xy_ref[pl.ds(r0, rows), :], wgu_ref, wd_ref)
        xy_ref[pl.ds(r0, rows), :] = (ye * gate_col(first, rows)).astype(BF16)

    @pl.when(fast & (m > 0))
    def _():
        ffn_rows(0, MOE_HEAD)

    @pl.when(fast & (m > MOE_HEAD))
    def _():
        ffn_rows(MOE_HEAD, MOE_CAP - MOE_HEAD)

    @pl.when(fast & (e_id == N_EXPERTS - 1))
    def _combine():
        def win(w, c1):
            w0 = pl.multiple_of(w * MOE_WIN, MOE_WIN)

            def contribution(r):
                rows = slot_rows(w, r)
                y = jnp.concatenate([xy_ref[pl.ds(buf_off(ee, rows[ee]), MOE_SLOT), :]
                                     for ee in range(N_EXPERTS)], axis=0)
                return lax.dot_general(sel_matrix(w, rows), y, (((0,), (0,)), ((), ())),
                                       preferred_element_type=F32)

            o_ref[pl.ds(w0, MOE_WIN), :] = contribution(0)

            def rnd(r, c2):
                o_ref[pl.ds(w0, MOE_WIN), :] += contribution(r)
                return c2

            lax.fori_loop(1, win_rounds(w), rnd, 0)
            return c1

        lax.fori_loop(0, N_WIN, win, 0)

    @pl.when(jnp.logical_not(fast))
    def _general():
        @pl.when(e_id == 0)
        def _():
            o_ref[...] = jnp.zeros_like(o_ref)

        sub_sb = lax.broadcasted_iota(I32, (MOE_SB, MOE_TILE), 0).astype(F32)

        def superblock(si, carry):
            lo = si * MOE_SB
            kp_all = kp_ref[pl.ds(e_id, 1), :]
            p = jnp.where(kp_all == sub_sb + (lo + 1).astype(F32), 1.0, 0.0).astype(BF16)
            x_ref[...] = jnp.dot(p, h_ref[...], preferred_element_type=F32).astype(BF16)
            yg_ref[...] = jnp.zeros_like(yg_ref)

            def block(bi, c2):
                r0 = pl.multiple_of(bi * MOE_BLK, MOE_BLK)
                ye = _swiglu(x_ref[pl.ds(r0, MOE_BLK), :], wgu_ref, wd_ref)
                yg_ref[pl.ds(r0, MOE_BLK), :] = (ye * gate_col(lo + r0, MOE_BLK)).astype(BF16)
                return c2

            nblk = jnp.minimum((m - lo + MOE_BLK - 1) // MOE_BLK, MOE_SB // MOE_BLK)
            lax.fori_loop(0, nblk, block, 0)
            o_ref[...] += lax.dot_general(p, yg_ref[...], (((0,), (0,)), ((), ())),
                                          preferred_element_type=F32)
            return carry

        lax.fori_loop(0, (m + MOE_SB - 1) // MOE_SB, superblock, 0)


def _moe_ffn(h, kp, aff_t, cb, wgu, wd):
    n, d = h.shape
    nt = n // MOE_TILE
    f = wd.shape[1]
    return pl.pallas_call(
        _ffn_kernel,
        out_shape=jax.ShapeDtypeStruct((n, d), F32),
        grid=(nt, N_EXPERTS),
        in_specs=[pl.BlockSpec(memory_space=pltpu.SMEM),
                  pl.BlockSpec((N_EXPERTS, MOE_TILE), lambda t, e: (0, t)),
                  pl.BlockSpec((N_EXPERTS, MOE_TILE), lambda t, e: (0, t)),
                  pl.BlockSpec((MOE_TILE, d), lambda t, e: (t, 0)),
                  pl.BlockSpec((1, d, 2 * f), lambda t, e: (e, 0, 0)),
                  pl.BlockSpec((1, f, d), lambda t, e: (e, 0, 0))],
        out_specs=pl.BlockSpec((MOE_TILE, d), lambda t, e: (t, 0)),
        scratch_shapes=[pltpu.VMEM((N_EXPERTS * MOE_CAP + MOE_SLOT, d), BF16),
                        pltpu.VMEM((MOE_SB, d), BF16), pltpu.VMEM((MOE_SB, d), BF16)],
        compiler_params=_cparams(("arbitrary", "arbitrary")), name="ffn",
    )(cb, kp, aff_t, h, wgu, wd)


def _moe(h, aff_t, wgu, wd):
    n = h.shape[0]
    cap = max(1, EC_CAPACITY * n // N_EXPERTS)
    tau, need = _thresholds(aff_t, cap)
    kp, cb = _select(aff_t, tau, need)
    return _moe_ffn(h, kp, aff_t, cb[:, :, :CB_STRIDE].reshape(-1), wgu, wd)


def _final_kernel(x_ref, m_ref, gt_ref, g_ref, o_ref):
    x = x_ref[...] + gt_ref[0] * m_ref[...]
    ms = jnp.mean(x * x, axis=-1, keepdims=True)
    o_ref[...] = (x * lax.rsqrt(ms + EPS)) * g_ref[...]


def _final(x, moe, gt, g, seq):
    n, d = x.shape
    t = FINAL_TILE
    tpb = seq // t
    tok = pl.BlockSpec((t, d), lambda i: (i, 0))
    return pl.pallas_call(
        _final_kernel, out_shape=jax.ShapeDtypeStruct((n, d), F32), grid=(n // t,),
        in_specs=[tok, tok, pl.BlockSpec((1, 1, d), lambda i: (i // tpb, 0, 0)),
                  pl.BlockSpec((1, d), lambda i: (0, 0))],
        out_specs=tok, compiler_params=_cparams(("arbitrary",)), name="final",
    )(x, moe, gt, g.reshape(1, d))


def _trunk(x3, mods, p):
    bsz, seq, d = x3.shape
    n = bsz * seq
    x = x3.reshape(n, d)
    tables = _rope_tables(seq)
    scale = HEAD_DIM ** -0.5
    sh1, sc1, gt1, sh2, sc2, gt2 = [m[:, None, :] for m in jnp.split(mods[0], 6, axis=-1)]
    qw, kw = A_HEADS * HEAD_DIM, A_KV_HEADS * HEAD_DIM
    pieces = ((0, qw, True, scale, 1, False), (qw, kw, True, 1.0, 1, True),
              (qw + kw, kw, False, 1.0, 1, True))
    q, k, v = _qkv(x, None, p["g_mix"][0], sh1, sc1, p["w_qkv_a"], tables, seq, pieces, TOK_TILE_A)
    (o,) = _banded_attention(
        q.reshape(bsz, seq, qw), k.reshape(bsz, seq, 2 * kw), v.reshape(bsz, seq, 2 * kw),
        bsz=bsz, seq_n=seq, dil=1, q_w=qw, k_w=2 * kw, win=A_WINDOW,
        pairs_per_kgroup=A_HEADS // A_KV_HEADS // 2, sink=p["sink_a"], want_lse=False,
        out_dtype=BF16)
    x, h, aff_t = _oproj([o.reshape(n, qw)], None, (1,), x, p["w_o_a"], gt1, p["g_ffn"][0],
                         sh2, sc2, p["w_router"][0], p["b_router"][0], seq, TOK_TILE_A)
    moe = _moe(h, aff_t, p["w_gate_up"][0], p["w_down"][0])
    gt2_prev = gt2
    sh1, sc1, gt1, sh2, sc2, gt2 = [m[:, None, :] for m in jnp.split(mods[1], 6, axis=-1)]
    ng = len(B_GROUPS)
    dils = tuple(dil for _, dil in B_GROUPS)
    pieces = tuple((part * ng * B_GW + gi * B_GW, B_GW, part < 2, scale if part == 0 else 1.0, dils[gi], False)
                   for gi in range(ng) for part in range(3))
    res = _qkv(x, (moe, gt2_prev), p["g_mix"][1], sh1, sc1, p["w_qkv_b"], tables, seq, pieces, TOK_TILE)
    x = res[0]
    outs, lses = [], []
    for gi, (window, dil) in enumerate(B_GROUPS):
        sn = seq // dil
        qg, kg, vg = [r.reshape(bsz, sn, dil * B_GW) for r in res[1 + 3 * gi:4 + 3 * gi]]
        og, lg = _banded_attention(
            qg, kg, vg, bsz=bsz, seq_n=sn, dil=dil, q_w=B_GW, k_w=B_GW,
            win=(window // 2) // dil, pairs_per_kgroup=1, sink=None, want_lse=True,
            out_dtype=BF16)
        outs.append(og.reshape(bsz * sn, dil * B_GW))
        lses.append(lg.reshape(bsz * sn, dil * B_GW))
    x, h, aff_t = _oproj(outs, lses, dils, x, p["w_o_b"], gt1, p["g_ffn"][1],
                         sh2, sc2, p["w_router"][1], p["b_router"][1], seq, TOK_TILE)
    moe = _moe(h, aff_t, p["w_gate_up"][1], p["w_down"][1])
    y = _final(x, moe, gt2, p["g_final"], seq)
    return y.reshape(bsz, seq, d)


def kernel(x_prompt, x_sample, c_prompt, c_sample, w_ada, b_ada, g_mix, g_ffn, w_qkv_a, sink_a, w_o_a, w_qkv_b, w_o_b, w_router, b_router, w_gate, w_up, w_down, g_final):
    bp = c_prompt.shape[0]
    bs = c_sample.shape[0]
    rows = -(-(bp + bs) // SUBLANES) * SUBLANES
    c_all = jnp.zeros((rows, D_MODEL), F32).at[:bp].set(c_prompt).at[bp:bp + bs].set(c_sample)
    mods = _ada(c_all, w_ada, b_ada)
    depth = w_ada.shape[0]
    wr_pad = jnp.zeros((depth, D_MODEL, LANES), F32).at[:, :, :N_EXPERTS].set(w_router).astype(BF16)
    br_pad = jnp.zeros((depth, 1, LANES), F32).at[:, 0, :N_EXPERTS].set(b_router)
    p = dict(
        g_mix=g_mix, g_ffn=g_ffn, g_final=g_final, sink_a=sink_a[0],
        w_qkv_a=w_qkv_a[0].astype(BF16), w_o_a=w_o_a[0].astype(BF16),
        w_qkv_b=w_qkv_b[0].astype(BF16), w_o_b=w_o_b[0].astype(BF16),
        w_router=wr_pad, b_router=br_pad,
        w_gate_up=jnp.concatenate([w_gate, w_up], axis=-1).astype(BF16), w_down=w_down.astype(BF16),
    )
    y_prompt = _trunk(x_prompt, mods[:, :bp], p)
    y_sample = _trunk(x_sample, mods[:, bp:bp + bs], p)
    return (y_prompt, y_sample)
```

```python
import functools

import jax
import jax.numpy as jnp
from jax import lax
from jax.experimental import pallas as pl
from jax.experimental.pallas import tpu as pltpu

F32 = jnp.float32
BF16 = jnp.bfloat16
I32 = jnp.int32

D_MODEL = 1024
HEAD_DIM = 64
A_HEADS = 16
A_KV_HEADS = 4
A_WINDOW = 128
B_GROUPS = ((128, 1), (512, 4), (2048, 16))
B_HEADS = 8
B_GW = B_HEADS * HEAD_DIM
N_EXPERTS = 16
EXPERT_FF = 1024
EC_CAPACITY = 2
ROPE_THETA = 10000.0
EPS = 1e-6
NEG = -1e30

LANES = 128
SUBLANES = 8
BF16_ROWS = 16
VMEM_LIMIT = 58 * 1024 * 1024

TOK_TILE = 512
TOK_TILE_A = 1024
FINAL_TILE = 2048
OPROJ_SPLIT = 2
ATT_TQ = 512
ATT_QB = 128
ATT_AHEAD = 3
MOE_TILE = 2048
MOE_CHUNK = 128
MOE_SB = 384
MOE_CAP = 384
MOE_WIN = 256
MOE_SLOT = 64
N_WIN = MOE_TILE // MOE_WIN
assert MOE_SLOT & (MOE_SLOT - 1) == 0 and MOE_SLOT % BF16_ROWS == 0 and MOE_CAP % MOE_SLOT == 0
CB_STRIDE = 32
MOE_BLK = 128
MOE_HEAD = 288
N_CHUNK = MOE_TILE // MOE_CHUNK


def _cparams(sem):
    return pltpu.CompilerParams(dimension_semantics=sem, vmem_limit_bytes=VMEM_LIMIT)


def _norm_mod(x, g, sh, sc):
    ms = jnp.mean(x * x, axis=-1, keepdims=True)
    y = (x * lax.rsqrt(ms + EPS)) * g
    return y * (1.0 + sc) + sh


def _ada_kernel(c_ref, w_ref, b_ref, o_ref):
    s = jax.nn.silu(c_ref[...])
    o_ref[0] = jnp.dot(s, w_ref[0], preferred_element_type=F32,
                       precision=lax.Precision.HIGHEST) + b_ref[0]


def _ada(c_all, w_ada, b_ada):
    depth, d, n6 = w_ada.shape
    rows = c_all.shape[0]
    tn = 1536
    return pl.pallas_call(
        _ada_kernel,
        out_shape=jax.ShapeDtypeStruct((depth, rows, n6), F32),
        grid=(depth, n6 // tn),
        in_specs=[
            pl.BlockSpec((rows, d), lambda l, j: (0, 0)),
            pl.BlockSpec((1, d, tn), lambda l, j: (l, 0, j)),
            pl.BlockSpec((1, 1, tn), lambda l, j: (l, 0, j)),
        ],
        out_specs=pl.BlockSpec((1, rows, tn), lambda l, j: (l, 0, j)),
        compiler_params=_cparams(("arbitrary", "arbitrary")),
        name="ada",
    )(c_all, w_ada, b_ada.reshape(depth, 1, n6))


def _rope_tables(seq):
    half = HEAD_DIM // 2
    inv = ROPE_THETA ** (-jnp.arange(half, dtype=F32) / half)
    ang = jnp.arange(seq).astype(F32)[:, None] * inv[None, :]
    cos = jnp.cos(ang)
    sin = jnp.sin(ang)
    zero = jnp.zeros_like(sin)
    c = jnp.concatenate([cos, cos, cos, cos], axis=1)
    a = jnp.concatenate([-sin, zero, -sin, zero], axis=1)
    b = jnp.concatenate([zero, sin, zero, sin], axis=1)
    return c, a, b


def _rope_group(xg, c, a, b):
    return xg * c + pltpu.roll(xg, LANES - HEAD_DIM // 2, 1) * a + pltpu.roll(xg, HEAD_DIM // 2, 1) * b


def _qkv_kernel(*refs, pieces, has_res):
    t = refs[0].shape[0]
    if has_res:
        x_ref, m_ref, gt_ref, g_ref, sh_ref, sc_ref, w_ref, c_ref, a_ref, b_ref, xo_ref = refs[:11]
        out_refs = refs[11:11 + len(pieces)]
        x = x_ref[...] + gt_ref[0] * m_ref[...]
        xo_ref[...] = x
    else:
        x_ref, g_ref, sh_ref, sc_ref, w_ref, c_ref, a_ref, b_ref = refs[:8]
        out_refs = refs[8:8 + len(pieces)]
        x = x_ref[...]
    s_ref = refs[-1]
    h = _norm_mod(x, g_ref[...], sh_ref[0], sc_ref[0]).astype(BF16)
    acc = jnp.dot(h, w_ref[...], preferred_element_type=F32)
    c = c_ref[...]
    a = a_ref[...]
    b = b_ref[...]
    slot = 0
    low_half = lax.broadcasted_iota(I32, (1, LANES), 1) < HEAD_DIM
    for (c0, w, rope, scale, dil, dup), o_ref in zip(pieces, out_refs):
        ng = w // LANES
        for j in range(ng):
            xg = acc[:, c0 + j * LANES:c0 + (j + 1) * LANES]
            if rope:
                xg = _rope_group(xg, c, a, b)
            if scale != 1.0:
                xg = xg * scale
            if dup:
                sw = pltpu.roll(xg, HEAD_DIM, 1)
                o_ref[:, 2 * j * LANES:(2 * j + 1) * LANES] = jnp.where(low_half, xg, sw).astype(BF16)
                o_ref[:, (2 * j + 1) * LANES:(2 * j + 2) * LANES] = jnp.where(low_half, sw, xg).astype(BF16)
            elif dil == 1:
                o_ref[:, j * LANES:(j + 1) * LANES] = xg.astype(BF16)
            else:
                s_ref[slot, j] = xg
        if dil > 1:
            rows = t // dil
            for r in range(dil):
                for j in range(ng):
                    o_ref[:, r * w + j * LANES:r * w + (j + 1) * LANES] = (
                        s_ref.at[slot, j][pl.ds(r, rows, stride=dil), :].astype(BF16))
            slot += 1


def _qkv(x, res, g, sh, sc, w_bf16, tables, seq, pieces, t):
    n, d = x.shape
    tpb = seq // t
    c, a, b = tables
    tok = pl.BlockSpec((t, d), lambda i: (i, 0))
    vec = pl.BlockSpec((1, 1, d), lambda i: (i // tpb, 0, 0))
    tab = pl.BlockSpec((t, LANES), lambda i: (i % tpb, 0))
    in_specs = [tok]
    args = [x]
    if res is not None:
        in_specs += [tok, vec]
        args += [res[0], res[1]]
    in_specs += [pl.BlockSpec((1, d), lambda i: (0, 0)), vec, vec,
                 pl.BlockSpec(w_bf16.shape, lambda i: (0, 0)), tab, tab, tab]
    args += [g.reshape(1, d), sh, sc, w_bf16, c, a, b]
    widths = [dil * w * (2 if dup else 1) for (_, w, _, _, dil, dup) in pieces]
    out_shape = [jax.ShapeDtypeStruct((n // p[4], wd), BF16) for p, wd in zip(pieces, widths)]
    out_specs = [pl.BlockSpec((t // p[4], wd), lambda i: (i, 0)) for p, wd in zip(pieces, widths)]
    if res is not None:
        out_shape = [jax.ShapeDtypeStruct((n, d), F32)] + out_shape
        out_specs = [tok] + out_specs
    n_strided = max(1, sum(1 for p in pieces if p[4] > 1))
    kern = functools.partial(_qkv_kernel, pieces=pieces, has_res=res is not None)
    return pl.pallas_call(
        kern, out_shape=out_shape, grid=(n // t,), in_specs=in_specs, out_specs=out_specs,
        scratch_shapes=[pltpu.VMEM((n_strided, B_GW // LANES, t, LANES), F32)],
        compiler_params=_cparams(("arbitrary",)), name="qkv",
    )(*args)


def _attn_kernel(*refs, tq, win, n_pair, pairs_per_kgroup, seq_len, has_sink, want_lse):
    pos = 0
    if has_sink:
        sink_ref = refs[0]
        pos = 1
    q_ref, kp_ref, kc_ref, kn_ref, vp_ref, vc_ref, vn_ref = refs[pos:pos + 7]
    o_ref = refs[pos + 7]
    lse_ref = refs[pos + 8] if want_lse else None
    i = pl.program_id(2)
    qb = ATT_QB
    kw = qb + 2 * win
    k = jnp.concatenate([kp_ref[0], kc_ref[0], kn_ref[0]], axis=0)
    v = jnp.concatenate([vp_ref[0], vc_ref[0], vn_ref[0]], axis=0)
    low_half = lax.broadcasted_iota(I32, (1, LANES), 1) < HEAD_DIM
    r_io = lax.broadcasted_iota(I32, (qb, kw), 0)
    c_io = lax.broadcasted_iota(I32, (qb, kw), 1)
    band = (c_io >= r_io) & (c_io - r_io <= 2 * win)
    biases = []
    for j in range(tq // qb):
        kpos = i * tq - win + j * qb + c_io
        biases.append(jnp.where(band & (kpos >= 0) & (kpos < seq_len), 0.0, NEG))
    masked = []
    for gk in range(n_pair // pairs_per_kgroup):
        kg = k[:, gk * LANES:(gk + 1) * LANES]
        vg = v[:, gk * LANES:(gk + 1) * LANES]
        zero = jnp.zeros_like(kg)
        masked.append(((jnp.where(low_half, kg, zero), jnp.where(low_half, vg, zero)),
                       (jnp.where(low_half, zero, kg), jnp.where(low_half, zero, vg))))
    units = [(p, j) for p in range(n_pair) for j in range(tq // qb)]

    def scores(u):
        p, j = u
        qp = q_ref[0, j * qb:(j + 1) * qb, p * LANES:(p + 1) * LANES]
        return [lax.dot_general(qp, kh[j * qb:j * qb + kw], (((1,), (1,)), ((), ())),
                                preferred_element_type=F32) + biases[j]
                for kh, _ in masked[p // pairs_per_kgroup]]

    def finish(u, ss):
        p, j = u
        o_pair = None
        lses = []
        for half, s in enumerate(ss):
            vh = masked[p // pairs_per_kgroup][half][1]
            m = jnp.max(s, axis=-1, keepdims=True)
            if has_sink:
                sk = sink_ref[2 * p + half]
                m = jnp.maximum(m, sk)
            e = jnp.exp(s - m)
            l = jnp.sum(e, axis=-1, keepdims=True)
            if has_sink:
                l = l + jnp.exp(sk - m)
            o_h = jnp.dot(e.astype(BF16), vh[j * qb:j * qb + kw], preferred_element_type=F32) / l
            o_pair = o_h if o_pair is None else o_pair + o_h
            lses.append(m + jnp.log(l))
        o_ref[0, j * qb:(j + 1) * qb, p * LANES:(p + 1) * LANES] = o_pair.astype(o_ref.dtype)
        if want_lse:
            lse_ref[0, j * qb:(j + 1) * qb, p * LANES:(p + 1) * LANES] = jnp.where(
                low_half, lses[0], lses[1])

    pending = [scores(u) for u in units[:ATT_AHEAD]]
    for idx, u in enumerate(units):
        if idx + ATT_AHEAD < len(units):
            pending.append(scores(units[idx + ATT_AHEAD]))
        finish(u, pending.pop(0))


def _banded_attention(q, k, v, *, bsz, seq_n, dil, q_w, k_w,
                      win, pairs_per_kgroup, sink, want_lse, out_dtype):
    tq = min(ATT_TQ, seq_n)
    nt = seq_n // tq
    wpt = tq // win
    nwb = seq_n // win

    def qmap(b, r, i):
        return (b, i, r)

    def kprev(b, r, i):
        return (b, jnp.maximum(i * wpt - 1, 0), r)

    def knext(b, r, i):
        return (b, jnp.minimum((i + 1) * wpt, nwb - 1), r)

    in_specs = []
    args = []
    if sink is not None:
        in_specs.append(pl.BlockSpec(memory_space=pltpu.SMEM))
        args.append(sink)
    in_specs += [pl.BlockSpec((1, tq, q_w), qmap),
                 pl.BlockSpec((1, win, k_w), kprev), pl.BlockSpec((1, tq, k_w), qmap),
                 pl.BlockSpec((1, win, k_w), knext),
                 pl.BlockSpec((1, win, k_w), kprev), pl.BlockSpec((1, tq, k_w), qmap),
                 pl.BlockSpec((1, win, k_w), knext)]
    args += [q, k, k, k, v, v, v]
    out_shape = [jax.ShapeDtypeStruct((bsz, seq_n, dil * q_w), out_dtype)]
    out_specs = [pl.BlockSpec((1, tq, q_w), qmap)]
    if want_lse:
        out_shape.append(jax.ShapeDtypeStruct((bsz, seq_n, dil * q_w), F32))
        out_specs.append(pl.BlockSpec((1, tq, q_w), qmap))
    kern = functools.partial(_attn_kernel, tq=tq, win=win, n_pair=q_w // LANES,
                             pairs_per_kgroup=pairs_per_kgroup, seq_len=seq_n,
                             has_sink=sink is not None, want_lse=want_lse)
    return pl.pallas_call(
        kern, out_shape=out_shape, grid=(bsz, dil, nt), in_specs=in_specs, out_specs=out_specs,
        compiler_params=_cparams(("arbitrary", "arbitrary", "arbitrary")), name="attn",
    )(*args)


def _oproj_kernel(*refs, dils):
    t = refs[0].shape[0] * dils[0]
    n_grp = len(dils)
    if n_grp == 1:
        o_refs, l_refs, rest = refs[:1], (), refs[1:]
    else:
        o_refs, l_refs, rest = refs[:n_grp], refs[n_grp:2 * n_grp], refs[2 * n_grp:]
    (x_ref, wo_ref, gt_ref, g_ref, sh_ref, sc_ref, wr_ref, br_ref,
     xo_ref, h_ref, at_ref, s_ref) = rest
    if n_grp == 1:
        o = o_refs[0][...]
    else:
        ng = B_GW // LANES

        def natural(ref, slot, dil):
            if dil == 1:
                return [ref[:, j * LANES:(j + 1) * LANES].astype(F32) for j in range(ng)]
            rows = t // dil
            for r in range(dil):
                for j in range(ng):
                    s_ref.at[slot, j][pl.ds(r, rows, stride=dil), :] = (
                        ref[:, r * B_GW + j * LANES:r * B_GW + (j + 1) * LANES].astype(F32))
            return [s_ref[slot, j] for j in range(ng)]

        os_ = [natural(o_refs[gi], 2 * gi, dils[gi]) for gi in range(n_grp)]
        ls_ = [natural(l_refs[gi], 2 * gi + 1, dils[gi]) for gi in range(n_grp)]
        cols = []
        for j in range(ng):
            ls = [ls_[gi][j] for gi in range(n_grp)]
            mx = ls[0]
            for v in ls[1:]:
                mx = jnp.maximum(mx, v)
            es = [jnp.exp(v - mx) for v in ls]
            den = es[0]
            for v in es[1:]:
                den = den + v
            oj = (es[0] / den) * os_[0][j]
            for gi in range(1, n_grp):
                oj = oj + (es[gi] / den) * os_[gi][j]
            cols.append(oj.astype(BF16))
        o = jnp.concatenate(cols, axis=1)
    hr = t // OPROJ_SPLIT
    ys = [jnp.dot(o[i * hr:(i + 1) * hr], wo_ref[...], preferred_element_type=F32)
          for i in range(OPROJ_SPLIT)]
    hbs = []
    for i in range(OPROJ_SPLIT):
        x = x_ref[i * hr:(i + 1) * hr, :] + gt_ref[0] * ys[i]
        xo_ref[i * hr:(i + 1) * hr, :] = x
        hb = _norm_mod(x, g_ref[...], sh_ref[0], sc_ref[0]).astype(BF16)
        h_ref[i * hr:(i + 1) * hr, :] = hb
        hbs.append(hb)
    lgs = [jnp.dot(hb, wr_ref[...], preferred_element_type=F32) + br_ref[...] for hb in hbs]
    lane = lax.broadcasted_iota(I32, lgs[0].shape, 1)
    for i in range(OPROJ_SPLIT):
        logits = jnp.where(lane < N_EXPERTS, lgs[i], NEG)
        mx = jnp.max(logits, axis=-1, keepdims=True)
        ex = jnp.exp(logits - mx)
        aff = ex / jnp.sum(ex, axis=-1, keepdims=True)
        at_ref[:, i * hr:(i + 1) * hr] = aff.T[:N_EXPERTS, :]


def _oproj(os_, ls_, dils, x, wo_bf16, gt, g, sh, sc, wr_pad, br_pad, seq, t):
    n, d = x.shape
    tpb = seq // t
    n_grp = len(os_)
    tok = pl.BlockSpec((t, d), lambda i: (i, 0))
    gspecs = [pl.BlockSpec((t // dil, o.shape[1]), lambda i: (i, 0)) for o, dil in zip(os_, dils)]
    vec = pl.BlockSpec((1, 1, d), lambda i: (i // tpb, 0, 0))
    row = pl.BlockSpec((1, d), lambda i: (0, 0))
    in_specs = gspecs + (gspecs if n_grp > 1 else []) + [
        tok, pl.BlockSpec(wo_bf16.shape, lambda i: (0, 0)), vec, row, vec, vec,
        pl.BlockSpec(wr_pad.shape, lambda i: (0, 0)), pl.BlockSpec((1, LANES), lambda i: (0, 0))]
    args = list(os_) + (list(ls_) if n_grp > 1 else []) + [
        x, wo_bf16, gt, g.reshape(1, d), sh, sc, wr_pad, br_pad]
    out_shape = [jax.ShapeDtypeStruct((n, d), F32), jax.ShapeDtypeStruct((n, d), BF16),
                 jax.ShapeDtypeStruct((N_EXPERTS, n), F32)]
    out_specs = [tok, tok, pl.BlockSpec((N_EXPERTS, t), lambda i: (0, i))]
    return pl.pallas_call(
        functools.partial(_oproj_kernel, dils=tuple(dils)),
        out_shape=out_shape, grid=(n // t,), in_specs=in_specs, out_specs=out_specs,
        scratch_shapes=[pltpu.VMEM((2 * n_grp, B_GW // LANES, t, LANES), F32)],
        compiler_params=_cparams(("arbitrary",)), name="oproj",
    )(*args)


def _thr_kernel(a_ref, tau_ref, need_ref, *, cap):
    bits = pltpu.bitcast(a_ref[...], I32)

    def body(it, tau):
        cand = tau | jnp.left_shift(jnp.int32(1), 30 - it)
        cnt = jnp.sum(jnp.where(bits >= cand[:, :1], 1.0, 0.0), axis=1, keepdims=True)
        return jnp.where(cnt >= cap, cand, tau)

    tau = lax.fori_loop(0, 31, body, jnp.zeros((N_EXPERTS, LANES), I32))
    cgt = jnp.sum(jnp.where(bits > tau[:, :1], 1.0, 0.0), axis=1, keepdims=True)
    tau_ref[...] = tau
    need_ref[...] = jnp.broadcast_to(cap - cgt, (N_EXPERTS, LANES))


def _thresholds(aff_t, cap):
    e, n = aff_t.shape
    vm = pl.BlockSpec(memory_space=pltpu.VMEM)
    return pl.pallas_call(
        functools.partial(_thr_kernel, cap=float(cap)),
        out_shape=[jax.ShapeDtypeStruct((e, LANES), I32), jax.ShapeDtypeStruct((e, LANES), F32)],
        in_specs=[vm], out_specs=[vm, vm],
        compiler_params=pltpu.CompilerParams(vmem_limit_bytes=VMEM_LIMIT), name="thr",
    )(aff_t)


def _sel_kernel(at_ref, tau_ref, need_ref, kp_ref, cnt_ref, run_ref):
    @pl.when(pl.program_id(0) == 0)
    def _():
        run_ref[...] = jnp.zeros_like(run_ref)

    c = MOE_CHUNK
    tau = tau_ref[...]
    need = need_ref[...]
    r_io = lax.broadcasted_iota(I32, (c, c), 0)
    c_io = lax.broadcasted_iota(I32, (c, c), 1)
    upper = jnp.where(r_io < c_io, 1.0, 0.0).astype(BF16)

    lane16 = lax.broadcasted_iota(I32, (N_EXPERTS, LANES), 1)

    gts, eqs, eq_ins, eq_cnts = [], [], [], []
    for ci in range(N_CHUNK):
        bits = pltpu.bitcast(at_ref[:, ci * c:(ci + 1) * c], I32)
        gts.append(bits > tau)
        eqs.append(bits == tau)
        eqf = jnp.where(eqs[ci], 1.0, 0.0)
        eq_ins.append(jnp.dot(eqf.astype(BF16), upper, preferred_element_type=F32))
        eq_cnts.append(jnp.sum(eqf, axis=1, keepdims=True))
    run = run_ref[...]
    sels, cums, cnts = [], [], []
    for ci in range(N_CHUNK):
        sels.append(gts[ci] | (eqs[ci] & (eq_ins[ci] + run < need)))
        self_ = jnp.where(sels[ci], 1.0, 0.0)
        cums.append(jnp.dot(self_.astype(BF16), upper, preferred_element_type=F32))
        cnts.append(jnp.sum(self_, axis=1, keepdims=True))
        run = run + eq_cnts[ci]
    run_ref[...] = run
    base = jnp.zeros((N_EXPERTS, LANES), F32)
    table = jnp.zeros((N_EXPERTS, LANES), F32)
    for ci in range(N_CHUNK):
        table = jnp.where(lane16 == ci, base, table)
        kp_ref[:, ci * c:(ci + 1) * c] = jnp.where(sels[ci], cums[ci] + base + 1.0, 0.0)
        base = base + cnts[ci]
    cnt_ref[0] = jnp.where(lane16 == N_CHUNK, base, table).astype(I32)


def _select(aff_t, tau, need):
    e, n = aff_t.shape
    nt = n // MOE_TILE
    full = pl.BlockSpec((e, LANES), lambda i: (0, 0))
    return pl.pallas_call(
        _sel_kernel,
        out_shape=[jax.ShapeDtypeStruct((e, n), F32),
                   jax.ShapeDtypeStruct((nt, e, LANES), I32)],
        grid=(nt,),
        in_specs=[pl.BlockSpec((e, MOE_TILE), lambda i: (0, i)), full, full],
        out_specs=[pl.BlockSpec((e, MOE_TILE), lambda i: (0, i)),
                   pl.BlockSpec((1, e, LANES), lambda i: (i, 0, 0))],
        scratch_shapes=[pltpu.VMEM((e, LANES), F32)],
        compiler_params=_cparams(("arbitrary",)), name="sel",
    )(aff_t, tau, need)


def _swiglu(x, wgu_ref, wd_ref):
    f = wd_ref.shape[1]
    parts = 2 if x.shape[0] >= 256 else 1
    pr = x.shape[0] // parts
    hgus = [jnp.dot(x[i * pr:(i + 1) * pr], wgu_ref[0], preferred_element_type=F32)
            for i in range(parts)]
    outs = []
    for hgu in hgus:
        hid = (jax.nn.silu(hgu[:, :f]) * hgu[:, f:]).astype(BF16)
        outs.append(jnp.dot(hid, wd_ref[0], preferred_element_type=F32))
    return outs[0] if parts == 1 else jnp.concatenate(outs, axis=0)


def _ffn_kernel(cb_ref, kp_ref, at_ref, h_ref, wgu_ref, wd_ref, o_ref,
                xy_ref, x_ref, yg_ref):
    t_id = pl.program_id(0)
    e_id = pl.program_id(1)

    def cb(ee, lane):
        return cb_ref[(t_id * N_EXPERTS + ee) * CB_STRIDE + lane]

    m = cb(e_id, N_CHUNK)
    m_max = cb(0, N_CHUNK)
    for ee in range(1, N_EXPERTS):
        m_max = jnp.maximum(m_max, cb(ee, N_CHUNK))
    fast = m_max <= MOE_CAP

    def gate_col(first_slot, rows):
        sub = lax.broadcasted_iota(I32, (rows, MOE_TILE), 0) + (first_slot + 1)
        hit = kp_ref[pl.ds(e_id, 1), :] == sub.astype(F32)
        return jnp.sum(jnp.where(hit, at_ref[pl.ds(e_id, 1), :], 0.0), axis=1, keepdims=True)

    cpw = MOE_WIN // MOE_CHUNK

    def slot_start(ee, w):
        return cb(ee, cpw * w) & -BF16_ROWS

    def win_rounds(w):
        span = jnp.int32(0)
        for ee in range(N_EXPERTS):
            span = jnp.maximum(span, cb(ee, cpw * w + cpw) - slot_start(ee, w))
        return lax.shift_right_logical(span + (MOE_SLOT - 1), jnp.int32(MOE_SLOT.bit_length() - 1))

    def slot_rows(w, r):
        return [jnp.minimum(slot_start(ee, w) + r * MOE_SLOT, MOE_CAP) for ee in range(N_EXPERTS)]

    def sel_matrix(w, rows):
        kpw = kp_ref[:, pl.ds(pl.multiple_of(w * MOE_WIN, MOE_WIN), MOE_WIN)]
        sub = lax.broadcasted_iota(I32, (MOE_SLOT, MOE_WIN), 0).astype(F32)
        parts = [jnp.where(kpw[ee:ee + 1, :] == sub + (rows[ee] + 1).astype(F32), 1.0, 0.0).astype(BF16)
                 for ee in range(N_EXPERTS)]
        return jnp.concatenate(parts, axis=0)

    def buf_off(ee, row):
        return pl.multiple_of(ee * MOE_CAP + row, BF16_ROWS)

    @pl.when(fast & (e_id == 0))
    def _dispatch():
        xy_ref[...] = jnp.zeros_like(xy_ref)

        def do_round(w, r):
            hw = h_ref[pl.ds(pl.multiple_of(w * MOE_WIN, MOE_WIN), MOE_WIN), :]
            rows = slot_rows(w, r)
            x = jnp.dot(sel_matrix(w, rows), hw, preferred_element_type=F32)
            for ee in range(N_EXPERTS):
                xy_ref[pl.ds(buf_off(ee, rows[ee]), MOE_SLOT), :] += (
                    x[ee * MOE_SLOT:(ee + 1) * MOE_SLOT, :].astype(BF16))

        def win(w, c1):
            do_round(w, 0)

            def rnd(r, c2):
                do_round(w, r)
                return c2

            lax.fori_loop(1, win_rounds(w), rnd, 0)
            return c1

        win(0, 0)
        lax.fori_loop(1, N_WIN, win, 0)

    def ffn_rows(first, rows):
        r0 = pl.multiple_of(e_id * MOE_CAP + first, BF16_ROWS)
        ye = _swiglu(xy_ref[pl.ds(r0, rows), :], wgu_ref, wd_ref)
        xy_ref[pl.ds(r0, rows), :] = (ye * gate_col(first, rows)).astype(BF16)

    @pl.when(fast & (m > 0))
    def _():
        ffn_rows(0, MOE_HEAD)

    @pl.when(fast & (m > MOE_HEAD))
    def _():
        ffn_rows(MOE_HEAD, MOE_CAP - MOE_HEAD)

    @pl.when(fast & (e_id == N_EXPERTS - 1))
    def _combine():
        def win(w, c1):
            w0 = pl.multiple_of(w * MOE_WIN, MOE_WIN)

            def contribution(r):
                rows = slot_rows(w, r)
                y = jnp.concatenate([xy_ref[pl.ds(buf_off(ee, rows[ee]), MOE_SLOT), :]
                                     for ee in range(N_EXPERTS)], axis=0)
                return lax.dot_general(sel_matrix(w, rows), y, (((0,), (0,)), ((), ())),
                                       preferred_element_type=F32)

            o_ref[pl.ds(w0, MOE_WIN), :] = contribution(0)

            def rnd(r, c2):
                o_ref[pl.ds(w0, MOE_WIN), :] += contribution(r)
                return c2

            lax.fori_loop(1, win_rounds(w), rnd, 0)
            return c1

        lax.fori_loop(0, N_WIN, win, 0)

    @pl.when(jnp.logical_not(fast))
    def _general():
        @pl.when(e_id == 0)
        def _():
            o_ref[...] = jnp.zeros_like(o_ref)

        sub_sb = lax.broadcasted_iota(I32, (MOE_SB, MOE_TILE), 0).astype(F32)

        def superblock(si, carry):
            lo = si * MOE_SB
            kp_all = kp_ref[pl.ds(e_id, 1), :]
            p = jnp.where(kp_all == sub_sb + (lo + 1).astype(F32), 1.0, 0.0).astype(BF16)
            x_ref[...] = jnp.dot(p, h_ref[...], preferred_element_type=F32).astype(BF16)
            yg_ref[...] = jnp.zeros_like(yg_ref)

            def block(bi, c2):
                r0 = pl.multiple_of(bi * MOE_BLK, MOE_BLK)
                ye = _swiglu(x_ref[pl.ds(r0, MOE_BLK), :], wgu_ref, wd_ref)
                yg_ref[pl.ds(r0, MOE_BLK), :] = (ye * gate_col(lo + r0, MOE_BLK)).astype(BF16)
                return c2

            nblk = jnp.minimum((m - lo + MOE_BLK - 1) // MOE_BLK, MOE_SB // MOE_BLK)
            lax.fori_loop(0, nblk, block, 0)
            o_ref[...] += lax.dot_general(p, yg_ref[...], (((0,), (0,)), ((), ())),
                                          preferred_element_type=F32)
            return carry

        lax.fori_loop(0, (m + MOE_SB - 1) // MOE_SB, superblock, 0)


def _moe_ffn(h, kp, aff_t, cb, wgu, wd):
    n, d = h.shape
    nt = n // MOE_TILE
    f = wd.shape[1]
    return pl.pallas_call(
        _ffn_kernel,
        out_shape=jax.ShapeDtypeStruct((n, d), F32),
        grid=(nt, N_EXPERTS),
        in_specs=[pl.BlockSpec(memory_space=pltpu.SMEM),
                  pl.BlockSpec((N_EXPERTS, MOE_TILE), lambda t, e: (0, t)),
                  pl.BlockSpec((N_EXPERTS, MOE_TILE), lambda t, e: (0, t)),
                  pl.BlockSpec((MOE_TILE, d), lambda t, e: (t, 0)),
                  pl.BlockSpec((1, d, 2 * f), lambda t, e: (e, 0, 0)),
                  pl.BlockSpec((1, f, d), lambda t, e: (e, 0, 0))],
        out_specs=pl.BlockSpec((MOE_TILE, d), lambda t, e: (t, 0)),
        scratch_shapes=[pltpu.VMEM((N_EXPERTS * MOE_CAP + MOE_SLOT, d), BF16),
                        pltpu.VMEM((MOE_SB, d), BF16), pltpu.VMEM((MOE_SB, d), BF16)],
        compiler_params=_cparams(("arbitrary", "arbitrary")), name="ffn",
    )(cb, kp, aff_t, h, wgu, wd)


def _moe(h, aff_t, wgu, wd):
    n = h.shape[0]
    cap = max(1, EC_CAPACITY * n // N_EXPERTS)
    tau, need = _thresholds(aff_t, cap)
    kp, cb = _select(aff_t, tau, need)
    return _moe_ffn(h, kp, aff_t, cb[:, :, :CB_STRIDE].reshape(-1), wgu, wd)


def _final_kernel(x_ref, m_ref, gt_ref, g_ref, o_ref):
    x = x_ref[...] + gt_ref[0] * m_ref[...]
    ms = jnp.mean(x * x, axis=-1, keepdims=True)
    o_ref[...] = (x * lax.rsqrt(ms + EPS)) * g_ref[...]


def _final(x, moe, gt, g, seq):
    n, d = x.shape
    t = FINAL_TILE
    tpb = seq // t
    tok = pl.BlockSpec((t, d), lambda i: (i, 0))
    return pl.pallas_call(
        _final_kernel, out_shape=jax.ShapeDtypeStruct((n, d), F32), grid=(n // t,),
        in_specs=[tok, tok, pl.BlockSpec((1, 1, d), lambda i: (i // tpb, 0, 0)),
                  pl.BlockSpec((1, d), lambda i: (0, 0))],
        out_specs=tok, compiler_params=_cparams(("arbitrary",)), name="final",
    )(x, moe, gt, g.reshape(1, d))


def _trunk(x3, mods, p):
    bsz, seq, d = x3.shape
    n = bsz * seq
    x = x3.reshape(n, d)
    tables = _rope_tables(seq)
    scale = HEAD_DIM ** -0.5
    sh1, sc1, gt1, sh2, sc2, gt2 = [m[:, None, :] for m in jnp.split(mods[0], 6, axis=-1)]
    qw, kw = A_HEADS * HEAD_DIM, A_KV_HEADS * HEAD_DIM
    pieces = ((0, qw, True, scale, 1, False), (qw, kw, True, 1.0, 1, True),
              (qw + kw, kw, False, 1.0, 1, True))
    q, k, v = _qkv(x, None, p["g_mix"][0], sh1, sc1, p["w_qkv_a"], tables, seq, pieces, TOK_TILE_A)
    (o,) = _banded_attention(
        q.reshape(bsz, seq, qw), k.reshape(bsz, seq, 2 * kw), v.reshape(bsz, seq, 2 * kw),
        bsz=bsz, seq_n=seq, dil=1, q_w=qw, k_w=2 * kw, win=A_WINDOW,
        pairs_per_kgroup=A_HEADS // A_KV_HEADS // 2, sink=p["sink_a"], want_lse=False,
        out_dtype=BF16)
    x, h, aff_t = _oproj([o.reshape(n, qw)], None, (1,), x, p["w_o_a"], gt1, p["g_ffn"][0],
                         sh2, sc2, p["w_router"][0], p["b_router"][0], seq, TOK_TILE_A)
    moe = _moe(h, aff_t, p["w_gate_up"][0], p["w_down"][0])
    gt2_prev = gt2
    sh1, sc1, gt1, sh2, sc2, gt2 = [m[:, None, :] for m in jnp.split(mods[1], 6, axis=-1)]
    ng = len(B_GROUPS)
    dils = tuple(dil for _, dil in B_GROUPS)
    pieces = tuple((part * ng * B_GW + gi * B_GW, B_GW, part < 2, scale if part == 0 else 1.0, dils[gi], False)
                   for gi in range(ng) for part in range(3))
    res = _qkv(x, (moe, gt2_prev), p["g_mix"][1], sh1, sc1, p["w_qkv_b"], tables, seq, pieces, TOK_TILE)
    x = res[0]
    outs, lses = [], []
    for gi, (window, dil) in enumerate(B_GROUPS):
        sn = seq // dil
        qg, kg, vg = [r.reshape(bsz, sn, dil * B_GW) for r in res[1 + 3 * gi:4 + 3 * gi]]
        og, lg = _banded_attention(
            qg, kg, vg, bsz=bsz, seq_n=sn, dil=dil, q_w=B_GW, k_w=B_GW,
            win=(window // 2) // dil, pairs_per_kgroup=1, sink=None, want_lse=True,
            out_dtype=BF16)
        outs.append(og.reshape(bsz * sn, dil * B_GW))
        lses.append(lg.reshape(bsz * sn, dil * B_GW))
    x, h, aff_t = _oproj(outs, lses, dils, x, p["w_o_b"], gt1, p["g_ffn"][1],
                         sh2, sc2, p["w_router"][1], p["b_router"][1], seq, TOK_TILE)
    moe = _moe(h, aff_t, p["w_gate_up"][1], p["w_down"][1])
    y = _final(x, moe, gt2, p["g_final"], seq)
    return y.reshape(bsz, seq, d)


def kernel(x_prompt, x_sample, c_prompt, c_sample, w_ada, b_ada, g_mix, g_ffn, w_qkv_a, sink_a, w_o_a, w_qkv_b, w_o_b, w_router, b_router, w_gate, w_up, w_down, g_final):
    bp = c_prompt.shape[0]
    bs = c_sample.shape[0]
    rows = -(-(bp + bs) // SUBLANES) * SUBLANES
    c_all = jnp.zeros((rows, D_MODEL), F32).at[:bp].set(c_prompt).at[bp:bp + bs].set(c_sample)
    mods = _ada(c_all, w_ada, b_ada)
    depth = w_ada.shape[0]
    wr_pad = jnp.zeros((depth, D_MODEL, LANES), F32).at[:, :, :N_EXPERTS].set(w_router).astype(BF16)
    br_pad = jnp.zeros((depth, 1, LANES), F32).at[:, 0, :N_EXPERTS].set(b_router)
    p = dict(
        g_mix=g_mix, g_ffn=g_ffn, g_final=g_final, sink_a=sink_a[0],
        w_qkv_a=w_qkv_a[0].astype(BF16), w_o_a=w_o_a[0].astype(BF16),
        w_qkv_b=w_qkv_b[0].astype(BF16), w_o_b=w_o_b[0].astype(BF16),
        w_router=wr_pad, b_router=br_pad,
        w_gate_up=jnp.concatenate([w_gate, w_up], axis=-1).astype(BF16), w_down=w_down.astype(BF16),
    )
    y_prompt = _trunk(x_prompt, mods[:, :bp], p)
    y_sample = _trunk(x_sample, mods[:, bp:bp + bs], p)
    return (y_prompt, y_sample)
```

```python
import functools

import jax
import jax.numpy as jnp
from jax import lax
from jax.experimental import pallas as pl
from jax.experimental.pallas import tpu as pltpu

F32 = jnp.float32
BF16 = jnp.bfloat16
I32 = jnp.int32

D_MODEL = 1024
HEAD_DIM = 64
A_HEADS = 16
A_KV_HEADS = 4
A_WINDOW = 128
B_GROUPS = ((128, 1), (512, 4), (2048, 16))
B_HEADS = 8
B_GW = B_HEADS * HEAD_DIM
N_EXPERTS = 16
EXPERT_FF = 1024
EC_CAPACITY = 2
ROPE_THETA = 10000.0
EPS = 1e-6
NEG = -1e30

LANES = 128
SUBLANES = 8
BF16_ROWS = 16
VMEM_LIMIT = 58 * 1024 * 1024

TOK_TILE = 512
TOK_TILE_A = 1024
OPROJ_TILE_A = 2048
FINAL_TILE = 2048
OPROJ_SPLIT = 2
ATT_TQ = 512
ATT_QB = 128
ATT_AHEAD = 3
MOE_TILE = 2048
MOE_CHUNK = 128
MOE_SB = 384
MOE_CAP = 384
MOE_WIN = 256
MOE_SLOT = 64
N_WIN = MOE_TILE // MOE_WIN
assert MOE_SLOT & (MOE_SLOT - 1) == 0 and MOE_SLOT % BF16_ROWS == 0 and MOE_CAP % MOE_SLOT == 0
CB_STRIDE = 32
MOE_BLK = 128
MOE_HEAD = 288
N_CHUNK = MOE_TILE // MOE_CHUNK


def _cparams(sem):
    return pltpu.CompilerParams(dimension_semantics=sem, vmem_limit_bytes=VMEM_LIMIT)


def _norm_mod(x, g, sh, sc):
    ms = jnp.mean(x * x, axis=-1, keepdims=True)
    y = (x * lax.rsqrt(ms + EPS)) * g
    return y * (1.0 + sc) + sh


def _ada_kernel(c_ref, w_ref, b_ref, o_ref):
    s = jax.nn.silu(c_ref[...])
    o_ref[0] = jnp.dot(s, w_ref[0], preferred_element_type=F32,
                       precision=lax.Precision.HIGHEST) + b_ref[0]


def _ada(c_all, w_ada, b_ada):
    depth, d, n6 = w_ada.shape
    rows = c_all.shape[0]
    tn = 1536
    return pl.pallas_call(
        _ada_kernel,
        out_shape=jax.ShapeDtypeStruct((depth, rows, n6), F32),
        grid=(depth, n6 // tn),
        in_specs=[
            pl.BlockSpec((rows, d), lambda l, j: (0, 0)),
            pl.BlockSpec((1, d, tn), lambda l, j: (l, 0, j)),
            pl.BlockSpec((1, 1, tn), lambda l, j: (l, 0, j)),
        ],
        out_specs=pl.BlockSpec((1, rows, tn), lambda l, j: (l, 0, j)),
        compiler_params=_cparams(("arbitrary", "arbitrary")),
        name="ada",
    )(c_all, w_ada, b_ada.reshape(depth, 1, n6))


def _rope_tables(seq):
    half = HEAD_DIM // 2
    inv = ROPE_THETA ** (-jnp.arange(half, dtype=F32) / half)
    ang = jnp.arange(seq).astype(F32)[:, None] * inv[None, :]
    cos = jnp.cos(ang)
    sin = jnp.sin(ang)
    zero = jnp.zeros_like(sin)
    c = jnp.concatenate([cos, cos, cos, cos], axis=1)
    a = jnp.concatenate([-sin, zero, -sin, zero], axis=1)
    b = jnp.concatenate([zero, sin, zero, sin], axis=1)
    return c, a, b


def _rope_group(xg, c, a, b):
    return xg * c + pltpu.roll(xg, LANES - HEAD_DIM // 2, 1) * a + pltpu.roll(xg, HEAD_DIM // 2, 1) * b


def _qkv_kernel(*refs, pieces, has_res):
    t = refs[0].shape[0]
    if has_res:
        x_ref, m_ref, gt_ref, g_ref, sh_ref, sc_ref, w_ref, c_ref, a_ref, b_ref, xo_ref = refs[:11]
        out_refs = refs[11:11 + len(pieces)]
        x = x_ref[...] + gt_ref[0] * m_ref[...]
        xo_ref[...] = x
    else:
        x_ref, g_ref, sh_ref, sc_ref, w_ref, c_ref, a_ref, b_ref = refs[:8]
        out_refs = refs[8:8 + len(pieces)]
        x = x_ref[...]
    s_ref = refs[-1]
    h = _norm_mod(x, g_ref[...], sh_ref[0], sc_ref[0]).astype(BF16)
    acc = jnp.dot(h, w_ref[...], preferred_element_type=F32)
    c = c_ref[...]
    a = a_ref[...]
    b = b_ref[...]
    slot = 0
    low_half = lax.broadcasted_iota(I32, (1, LANES), 1) < HEAD_DIM
    for (c0, w, rope, scale, dil, dup), o_ref in zip(pieces, out_refs):
        ng = w // LANES
        for j in range(ng):
            xg = acc[:, c0 + j * LANES:c0 + (j + 1) * LANES]
            if rope:
                xg = _rope_group(xg, c, a, b)
            if scale != 1.0:
                xg = xg * scale
            if dup:
                sw = pltpu.roll(xg, HEAD_DIM, 1)
                o_ref[:, 2 * j * LANES:(2 * j + 1) * LANES] = jnp.where(low_half, xg, sw).astype(BF16)
                o_ref[:, (2 * j + 1) * LANES:(2 * j + 2) * LANES] = jnp.where(low_half, sw, xg).astype(BF16)
            elif dil == 1:
                o_ref[:, j * LANES:(j + 1) * LANES] = xg.astype(BF16)
            else:
                s_ref[slot, j] = xg
        if dil > 1:
            rows = t // dil
            for r in range(dil):
                for j in range(ng):
                    o_ref[:, r * w + j * LANES:r * w + (j + 1) * LANES] = (
                        s_ref.at[slot, j][pl.ds(r, rows, stride=dil), :].astype(BF16))
            slot += 1


def _qkv(x, res, g, sh, sc, w_bf16, tables, seq, pieces, t):
    n, d = x.shape
    tpb = seq // t
    c, a, b = tables
    tok = pl.BlockSpec((t, d), lambda i: (i, 0))
    vec = pl.BlockSpec((1, 1, d), lambda i: (i // tpb, 0, 0))
    tab = pl.BlockSpec((t, LANES), lambda i: (i % tpb, 0))
    in_specs = [tok]
    args = [x]
    if res is not None:
        in_specs += [tok, vec]
        args += [res[0], res[1]]
    in_specs += [pl.BlockSpec((1, d), lambda i: (0, 0)), vec, vec,
                 pl.BlockSpec(w_bf16.shape, lambda i: (0, 0)), tab, tab, tab]
    args += [g.reshape(1, d), sh, sc, w_bf16, c, a, b]
    widths = [dil * w * (2 if dup else 1) for (_, w, _, _, dil, dup) in pieces]
    out_shape = [jax.ShapeDtypeStruct((n // p[4], wd), BF16) for p, wd in zip(pieces, widths)]
    out_specs = [pl.BlockSpec((t // p[4], wd), lambda i: (i, 0)) for p, wd in zip(pieces, widths)]
    if res is not None:
        out_shape = [jax.ShapeDtypeStruct((n, d), F32)] + out_shape
        out_specs = [tok] + out_specs
    n_strided = max(1, sum(1 for p in pieces if p[4] > 1))
    kern = functools.partial(_qkv_kernel, pieces=pieces, has_res=res is not None)
    return pl.pallas_call(
        kern, out_shape=out_shape, grid=(n // t,), in_specs=in_specs, out_specs=out_specs,
        scratch_shapes=[pltpu.VMEM((n_strided, B_GW // LANES, t, LANES), F32)],
        compiler_params=_cparams(("arbitrary",)), name="qkv",
    )(*args)


def _attn_kernel(*refs, tq, win, n_pair, pairs_per_kgroup, seq_len, has_sink, want_lse):
    pos = 0
    if has_sink:
        sink_ref = refs[0]
        pos = 1
    q_ref, kp_ref, kc_ref, kn_ref, vp_ref, vc_ref, vn_ref = refs[pos:pos + 7]
    o_ref = refs[pos + 7]
    lse_ref = refs[pos + 8] if want_lse else None
    i = pl.program_id(2)
    qb = ATT_QB
    kw = qb + 2 * win
    k = jnp.concatenate([kp_ref[0], kc_ref[0], kn_ref[0]], axis=0)
    v = jnp.concatenate([vp_ref[0], vc_ref[0], vn_ref[0]], axis=0)
    low_half = lax.broadcasted_iota(I32, (1, LANES), 1) < HEAD_DIM
    r_io = lax.broadcasted_iota(I32, (qb, kw), 0)
    c_io = lax.broadcasted_iota(I32, (qb, kw), 1)
    band = (c_io >= r_io) & (c_io - r_io <= 2 * win)
    biases = []
    for j in range(tq // qb):
        kpos = i * tq - win + j * qb + c_io
        biases.append(jnp.where(band & (kpos >= 0) & (kpos < seq_len), 0.0, NEG))
    masked = []
    for gk in range(n_pair // pairs_per_kgroup):
        kg = k[:, gk * LANES:(gk + 1) * LANES]
        vg = v[:, gk * LANES:(gk + 1) * LANES]
        zero = jnp.zeros_like(kg)
        masked.append(((jnp.where(low_half, kg, zero), jnp.where(low_half, vg, zero)),
                       (jnp.where(low_half, zero, kg), jnp.where(low_half, zero, vg))))
    units = [(p, j) for p in range(n_pair) for j in range(tq // qb)]

    def scores(u):
        p, j = u
        qp = q_ref[0, j * qb:(j + 1) * qb, p * LANES:(p + 1) * LANES]
        return [lax.dot_general(qp, kh[j * qb:j * qb + kw], (((1,), (1,)), ((), ())),
                                preferred_element_type=F32) + biases[j]
                for kh, _ in masked[p // pairs_per_kgroup]]

    def finish(u, ss):
        p, j = u
        o_pair = None
        lses = []
        for half, s in enumerate(ss):
            vh = masked[p // pairs_per_kgroup][half][1]
            m = jnp.max(s, axis=-1, keepdims=True)
            if has_sink:
                sk = sink_ref[2 * p + half]
                m = jnp.maximum(m, sk)
            e = jnp.exp(s - m)
            l = jnp.sum(e, axis=-1, keepdims=True)
            if has_sink:
                l = l + jnp.exp(sk - m)
            o_h = jnp.dot(e.astype(BF16), vh[j * qb:j * qb + kw], preferred_element_type=F32) / l
            o_pair = o_h if o_pair is None else o_pair + o_h
            lses.append(m + jnp.log(l))
        o_ref[0, j * qb:(j + 1) * qb, p * LANES:(p + 1) * LANES] = o_pair.astype(o_ref.dtype)
        if want_lse:
            lse_ref[0, j * qb:(j + 1) * qb, p * LANES:(p + 1) * LANES] = jnp.where(
                low_half, lses[0], lses[1])

    pending = [scores(u) for u in units[:ATT_AHEAD]]
    for idx, u in enumerate(units):
        if idx + ATT_AHEAD < len(units):
            pending.append(scores(units[idx + ATT_AHEAD]))
        finish(u, pending.pop(0))


def _banded_attention(q, k, v, *, bsz, seq_n, dil, q_w, k_w,
                      win, pairs_per_kgroup, sink, want_lse, out_dtype):
    tq = min(ATT_TQ, seq_n)
    nt = seq_n // tq
    wpt = tq // win
    nwb = seq_n // win

    def qmap(b, r, i):
        return (b, i, r)

    def kprev(b, r, i):
        return (b, jnp.maximum(i * wpt - 1, 0), r)

    def knext(b, r, i):
        return (b, jnp.minimum((i + 1) * wpt, nwb - 1), r)

    in_specs = []
    args = []
    if sink is not None:
        in_specs.append(pl.BlockSpec(memory_space=pltpu.SMEM))
        args.append(sink)
    in_specs += [pl.BlockSpec((1, tq, q_w), qmap),
                 pl.BlockSpec((1, win, k_w), kprev), pl.BlockSpec((1, tq, k_w), qmap),
                 pl.BlockSpec((1, win, k_w), knext),
                 pl.BlockSpec((1, win, k_w), kprev), pl.BlockSpec((1, tq, k_w), qmap),
                 pl.BlockSpec((1, win, k_w), knext)]
    args += [q, k, k, k, v, v, v]
    out_shape = [jax.ShapeDtypeStruct((bsz, seq_n, dil * q_w), out_dtype)]
    out_specs = [pl.BlockSpec((1, tq, q_w), qmap)]
    if want_lse:
        out_shape.append(jax.ShapeDtypeStruct((bsz, seq_n, dil * q_w), F32))
        out_specs.append(pl.BlockSpec((1, tq, q_w), qmap))
    kern = functools.partial(_attn_kernel, tq=tq, win=win, n_pair=q_w // LANES,
                             pairs_per_kgroup=pairs_per_kgroup, seq_len=seq_n,
                             has_sink=sink is not None, want_lse=want_lse)
    return pl.pallas_call(
        kern, out_shape=out_shape, grid=(bsz, dil, nt), in_specs=in_specs, out_specs=out_specs,
        compiler_params=_cparams(("arbitrary", "arbitrary", "arbitrary")), name="attn",
    )(*args)


def _oproj_kernel(*refs, dils):
    t = refs[0].shape[0] * dils[0]
    n_grp = len(dils)
    if n_grp == 1:
        o_refs, l_refs, rest = refs[:1], (), refs[1:]
    else:
        o_refs, l_refs, rest = refs[:n_grp], refs[n_grp:2 * n_grp], refs[2 * n_grp:]
    (x_ref, wo_ref, gt_ref, g_ref, sh_ref, sc_ref, wr_ref, br_ref,
     xo_ref, h_ref, at_ref, s_ref) = rest
    if n_grp == 1:
        o = o_refs[0][...]
    else:
        ng = B_GW // LANES

        def natural(ref, slot, dil):
            if dil == 1:
                return [ref[:, j * LANES:(j + 1) * LANES].astype(F32) for j in range(ng)]
            rows = t // dil
            for r in range(dil):
                for j in range(ng):
                    s_ref.at[slot, j][pl.ds(r, rows, stride=dil), :] = (
                        ref[:, r * B_GW + j * LANES:r * B_GW + (j + 1) * LANES].astype(F32))
            return [s_ref[slot, j] for j in range(ng)]

        os_ = [natural(o_refs[gi], 2 * gi, dils[gi]) for gi in range(n_grp)]
        ls_ = [natural(l_refs[gi], 2 * gi + 1, dils[gi]) for gi in range(n_grp)]
        cols = []
        for j in range(ng):
            ls = [ls_[gi][j] for gi in range(n_grp)]
            mx = ls[0]
            for v in ls[1:]:
                mx = jnp.maximum(mx, v)
            es = [jnp.exp(v - mx) for v in ls]
            den = es[0]
            for v in es[1:]:
                den = den + v
            oj = (es[0] / den) * os_[0][j]
            for gi in range(1, n_grp):
                oj = oj + (es[gi] / den) * os_[gi][j]
            cols.append(oj.astype(BF16))
        o = jnp.concatenate(cols, axis=1)
    hr = t // OPROJ_SPLIT
    ys = [jnp.dot(o[i * hr:(i + 1) * hr], wo_ref[...], preferred_element_type=F32)
          for i in range(OPROJ_SPLIT)]
    hbs = []
    for i in range(OPROJ_SPLIT):
        x = x_ref[i * hr:(i + 1) * hr, :] + gt_ref[0] * ys[i]
        xo_ref[i * hr:(i + 1) * hr, :] = x
        hb = _norm_mod(x, g_ref[...], sh_ref[0], sc_ref[0]).astype(BF16)
        h_ref[i * hr:(i + 1) * hr, :] = hb
        hbs.append(hb)
    lgs = [jnp.dot(hb, wr_ref[...], preferred_element_type=F32) + br_ref[...] for hb in hbs]
    lane = lax.broadcasted_iota(I32, lgs[0].shape, 1)
    for i in range(OPROJ_SPLIT):
        logits = jnp.where(lane < N_EXPERTS, lgs[i], NEG)
        mx = jnp.max(logits, axis=-1, keepdims=True)
        ex = jnp.exp(logits - mx)
        aff = ex / jnp.sum(ex, axis=-1, keepdims=True)
        at_ref[:, i * hr:(i + 1) * hr] = aff.T[:N_EXPERTS, :]


def _oproj(os_, ls_, dils, x, wo_bf16, gt, g, sh, sc, wr_pad, br_pad, seq, t):
    n, d = x.shape
    tpb = seq // t
    n_grp = len(os_)
    tok = pl.BlockSpec((t, d), lambda i: (i, 0))
    gspecs = [pl.BlockSpec((t // dil, o.shape[1]), lambda i: (i, 0)) for o, dil in zip(os_, dils)]
    vec = pl.BlockSpec((1, 1, d), lambda i: (i // tpb, 0, 0))
    row = pl.BlockSpec((1, d), lambda i: (0, 0))
    in_specs = gspecs + (gspecs if n_grp > 1 else []) + [
        tok, pl.BlockSpec(wo_bf16.shape, lambda i: (0, 0)), vec, row, vec, vec,
        pl.BlockSpec(wr_pad.shape, lambda i: (0, 0)), pl.BlockSpec((1, LANES), lambda i: (0, 0))]
    args = list(os_) + (list(ls_) if n_grp > 1 else []) + [
        x, wo_bf16, gt, g.reshape(1, d), sh, sc, wr_pad, br_pad]
    out_shape = [jax.ShapeDtypeStruct((n, d), F32), jax.ShapeDtypeStruct((n, d), BF16),
                 jax.ShapeDtypeStruct((N_EXPERTS, n), F32)]
    out_specs = [tok, tok, pl.BlockSpec((N_EXPERTS, t), lambda i: (0, i))]
    return pl.pallas_call(
        functools.partial(_oproj_kernel, dils=tuple(dils)),
        out_shape=out_shape, grid=(n // t,), in_specs=in_specs, out_specs=out_specs,
        scratch_shapes=[pltpu.VMEM((2 * n_grp, B_GW // LANES, t, LANES), F32)],
        compiler_params=_cparams(("arbitrary",)), name="oproj",
    )(*args)


def _thr_kernel(a_ref, tau_ref, need_ref, *, cap):
    bits = pltpu.bitcast(a_ref[...], I32)

    def body(it, tau):
        cand = tau | jnp.left_shift(jnp.int32(1), 30 - it)
        cnt = jnp.sum(jnp.where(bits >= cand[:, :1], 1.0, 0.0), axis=1, keepdims=True)
        return jnp.where(cnt >= cap, cand, tau)

    tau = lax.fori_loop(0, 31, body, jnp.zeros((N_EXPERTS, LANES), I32))
    cgt = jnp.sum(jnp.where(bits > tau[:, :1], 1.0, 0.0), axis=1, keepdims=True)
    tau_ref[...] = tau
    need_ref[...] = jnp.broadcast_to(cap - cgt, (N_EXPERTS, LANES))


def _thresholds(aff_t, cap):
    e, n = aff_t.shape
    vm = pl.BlockSpec(memory_space=pltpu.VMEM)
    return pl.pallas_call(
        functools.partial(_thr_kernel, cap=float(cap)),
        out_shape=[jax.ShapeDtypeStruct((e, LANES), I32), jax.ShapeDtypeStruct((e, LANES), F32)],
        in_specs=[vm], out_specs=[vm, vm],
        compiler_params=pltpu.CompilerParams(vmem_limit_bytes=VMEM_LIMIT), name="thr",
    )(aff_t)


def _sel_kernel(at_ref, tau_ref, need_ref, kp_ref, cnt_ref, run_ref):
    @pl.when(pl.program_id(0) == 0)
    def _():
        run_ref[...] = jnp.zeros_like(run_ref)

    c = MOE_CHUNK
    tau = tau_ref[...]
    need = need_ref[...]
    r_io = lax.broadcasted_iota(I32, (c, c), 0)
    c_io = lax.broadcasted_iota(I32, (c, c), 1)
    upper = jnp.where(r_io < c_io, 1.0, 0.0).astype(BF16)

    lane16 = lax.broadcasted_iota(I32, (N_EXPERTS, LANES), 1)

    gts, eqs, eq_ins, eq_cnts = [], [], [], []
    for ci in range(N_CHUNK):
        bits = pltpu.bitcast(at_ref[:, ci * c:(ci + 1) * c], I32)
        gts.append(bits > tau)
        eqs.append(bits == tau)
        eqf = jnp.where(eqs[ci], 1.0, 0.0)
        eq_ins.append(jnp.dot(eqf.astype(BF16), upper, preferred_element_type=F32))
        eq_cnts.append(jnp.sum(eqf, axis=1, keepdims=True))
    run = run_ref[...]
    sels, cums, cnts = [], [], []
    for ci in range(N_CHUNK):
        sels.append(gts[ci] | (eqs[ci] & (eq_ins[ci] + run < need)))
        self_ = jnp.where(sels[ci], 1.0, 0.0)
        cums.append(jnp.dot(self_.astype(BF16), upper, preferred_element_type=F32))
        cnts.append(jnp.sum(self_, axis=1, keepdims=True))
        run = run + eq_cnts[ci]
    run_ref[...] = run
    base = jnp.zeros((N_EXPERTS, LANES), F32)
    table = jnp.zeros((N_EXPERTS, LANES), F32)
    for ci in range(N_CHUNK):
        table = jnp.where(lane16 == ci, base, table)
        kp_ref[:, ci * c:(ci + 1) * c] = jnp.where(sels[ci], cums[ci] + base + 1.0, 0.0)
        base = base + cnts[ci]
    cnt_ref[0] = jnp.where(lane16 == N_CHUNK, base, table).astype(I32)


def _select(aff_t, tau, need):
    e, n = aff_t.shape
    nt = n // MOE_TILE
    full = pl.BlockSpec((e, LANES), lambda i: (0, 0))
    return pl.pallas_call(
        _sel_kernel,
        out_shape=[jax.ShapeDtypeStruct((e, n), F32),
                   jax.ShapeDtypeStruct((nt, e, LANES), I32)],
        grid=(nt,),
        in_specs=[pl.BlockSpec((e, MOE_TILE), lambda i: (0, i)), full, full],
        out_specs=[pl.BlockSpec((e, MOE_TILE), lambda i: (0, i)),
                   pl.BlockSpec((1, e, LANES), lambda i: (i, 0, 0))],
        scratch_shapes=[pltpu.VMEM((e, LANES), F32)],
        compiler_params=_cparams(("arbitrary",)), name="sel",
    )(aff_t, tau, need)


def _swiglu(x, wgu_ref, wd_ref):
    f = wd_ref.shape[1]
    parts = 2 if x.shape[0] >= 256 else 1
    pr = x.shape[0] // parts
    hgus = [jnp.dot(x[i * pr:(i + 1) * pr], wgu_ref[0], preferred_element_type=F32)
            for i in range(parts)]
    outs = []
    for hgu in hgus:
        hid = (jax.nn.silu(hgu[:, :f]) * hgu[:, f:]).astype(BF16)
        outs.append(jnp.dot(hid, wd_ref[0], preferred_element_type=F32))
    return outs[0] if parts == 1 else jnp.concatenate(outs, axis=0)


def _ffn_kernel(cb_ref, kp_ref, at_ref, h_ref, wgu_ref, wd_ref, o_ref,
                xy_ref, x_ref, yg_ref):
    t_id = pl.program_id(0)
    e_id = pl.program_id(1)

    def cb(ee, lane):
        return cb_ref[(t_id * N_EXPERTS + ee) * CB_STRIDE + lane]

    m = cb(e_id, N_CHUNK)
    m_max = cb(0, N_CHUNK)
    for ee in range(1, N_EXPERTS):
        m_max = jnp.maximum(m_max, cb(ee, N_CHUNK))
    fast = m_max <= MOE_CAP

    def gate_col(first_slot, rows):
        sub = lax.broadcasted_iota(I32, (rows, MOE_TILE), 0) + (first_slot + 1)
        hit = kp_ref[pl.ds(e_id, 1), :] == sub.astype(F32)
        return jnp.sum(jnp.where(hit, at_ref[pl.ds(e_id, 1), :], 0.0), axis=1, keepdims=True)

    cpw = MOE_WIN // MOE_CHUNK

    def slot_start(ee, w):
        return cb(ee, cpw * w) & -BF16_ROWS

    def win_rounds(w):
        span = jnp.int32(0)
        for ee in range(N_EXPERTS):
            span = jnp.maximum(span, cb(ee, cpw * w + cpw) - slot_start(ee, w))
        return lax.shift_right_logical(span + (MOE_SLOT - 1), jnp.int32(MOE_SLOT.bit_length() - 1))

    def slot_rows(w, r):
        return [jnp.minimum(slot_start(ee, w) + r * MOE_SLOT, MOE_CAP) for ee in range(N_EXPERTS)]

    def sel_matrix(w, rows):
        kpw = kp_ref[:, pl.ds(pl.multiple_of(w * MOE_WIN, MOE_WIN), MOE_WIN)]
        sub = lax.broadcasted_iota(I32, (MOE_SLOT, MOE_WIN), 0).astype(F32)
        parts = [jnp.where(kpw[ee:ee + 1, :] == sub + (rows[ee] + 1).astype(F32), 1.0, 0.0).astype(BF16)
                 for ee in range(N_EXPERTS)]
        return jnp.concatenate(parts, axis=0)

    def buf_off(ee, row):
        return pl.multiple_of(ee * MOE_CAP + row, BF16_ROWS)

    @pl.when(fast & (e_id == 0))
    def _dispatch():
        xy_ref[...] = jnp.zeros_like(xy_ref)

        def do_round(w, r):
            hw = h_ref[pl.ds(pl.multiple_of(w * MOE_WIN, MOE_WIN), MOE_WIN), :]
            rows = slot_rows(w, r)
            x = jnp.dot(sel_matrix(w, rows), hw, preferred_element_type=F32)
            for ee in range(N_EXPERTS):
                xy_ref[pl.ds(buf_off(ee, rows[ee]), MOE_SLOT), :] += (
                    x[ee * MOE_SLOT:(ee + 1) * MOE_SLOT, :].astype(BF16))

        def win(w, c1):
            do_round(w, 0)

            def rnd(r, c2):
                do_round(w, r)
                return c2

            lax.fori_loop(1, win_rounds(w), rnd, 0)
            return c1

        win(0, 0)
        lax.fori_loop(1, N_WIN, win, 0)

    def ffn_rows(first, rows):
        r0 = pl.multiple_of(e_id * MOE_CAP + first, BF16_ROWS)
        ye = _swiglu(xy_ref[pl.ds(r0, rows), :], wgu_ref, wd_ref)
        xy_ref[pl.ds(r0, rows), :] = (ye * gate_col(first, rows)).astype(BF16)

    @pl.when(fast & (m > 0))
    def _():
        ffn_rows(0, MOE_HEAD)

    @pl.when(fast & (m > MOE_HEAD))
    def _():
        ffn_rows(MOE_HEAD, MOE_CAP - MOE_HEAD)

    @pl.when(fast & (e_id == N_EXPERTS - 1))
    def _combine():
        def win(w, c1):
            w0 = pl.multiple_of(w * MOE_WIN, MOE_WIN)

            def contribution(r):
                rows = slot_rows(w, r)
                y = jnp.concatenate([xy_ref[pl.ds(buf_off(ee, rows[ee]), MOE_SLOT), :]
                                     for ee in range(N_EXPERTS)], axis=0)
                return lax.dot_general(sel_matrix(w, rows), y, (((0,), (0,)), ((), ())),
                                       preferred_element_type=F32)

            o_ref[pl.ds(w0, MOE_WIN), :] = contribution(0)

            def rnd(r, c2):
                o_ref[pl.ds(w0, MOE_WIN), :] += contribution(r)
                return c2

            lax.fori_loop(1, win_rounds(w), rnd, 0)
            return c1

        lax.fori_loop(0, N_WIN, win, 0)

    @pl.when(jnp.logical_not(fast))
    def _general():
        @pl.when(e_id == 0)
        def _():
            o_ref[...] = jnp.zeros_like(o_ref)

        sub_sb = lax.broadcasted_iota(I32, (MOE_SB, MOE_TILE), 0).astype(F32)

        def superblock(si, carry):
            lo = si * MOE_SB
            kp_all = kp_ref[pl.ds(e_id, 1), :]
            p = jnp.where(kp_all == sub_sb + (lo + 1).astype(F32), 1.0, 0.0).astype(BF16)
            x_ref[...] = jnp.dot(p, h_ref[...], preferred_element_type=F32).astype(BF16)
            yg_ref[...] = jnp.zeros_like(yg_ref)

            def block(bi, c2):
                r0 = pl.multiple_of(bi * MOE_BLK, MOE_BLK)
                ye = _swiglu(x_ref[pl.ds(r0, MOE_BLK), :], wgu_ref, wd_ref)
                yg_ref[pl.ds(r0, MOE_BLK), :] = (ye * gate_col(lo + r0, MOE_BLK)).astype(BF16)
                return c2

            nblk = jnp.minimum((m - lo + MOE_BLK - 1) // MOE_BLK, MOE_SB // MOE_BLK)
            lax.fori_loop(0, nblk, block, 0)
            o_ref[...] += lax.dot_general(p, yg_ref[...], (((0,), (0,)), ((), ())),
                                          preferred_element_type=F32)
            return carry

        lax.fori_loop(0, (m + MOE_SB - 1) // MOE_SB, superblock, 0)


def _moe_ffn(h, kp, aff_t, cb, wgu, wd):
    n, d = h.shape
    nt = n // MOE_TILE
    f = wd.shape[1]
    return pl.pallas_call(
        _ffn_kernel,
        out_shape=jax.ShapeDtypeStruct((n, d), F32),
        grid=(nt, N_EXPERTS),
        in_specs=[pl.BlockSpec(memory_space=pltpu.SMEM),
                  pl.BlockSpec((N_EXPERTS, MOE_TILE), lambda t, e: (0, t)),
                  pl.BlockSpec((N_EXPERTS, MOE_TILE), lambda t, e: (0, t)),
                  pl.BlockSpec((MOE_TILE, d), lambda t, e: (t, 0)),
                  pl.BlockSpec((1, d, 2 * f), lambda t, e: (e, 0, 0)),
                  pl.BlockSpec((1, f, d), lambda t, e: (e, 0, 0))],
        out_specs=pl.BlockSpec((MOE_TILE, d), lambda t, e: (t, 0)),
        scratch_shapes=[pltpu.VMEM((N_EXPERTS * MOE_CAP + MOE_SLOT, d), BF16),
                        pltpu.VMEM((MOE_SB, d), BF16), pltpu.VMEM((MOE_SB, d), BF16)],
        compiler_params=_cparams(("arbitrary", "arbitrary")), name="ffn",
    )(cb, kp, aff_t, h, wgu, wd)


def _moe(h, aff_t, wgu, wd):
    n = h.shape[0]
    cap = max(1, EC_CAPACITY * n // N_EXPERTS)
    tau, need = _thresholds(aff_t, cap)
    kp, cb = _select(aff_t, tau, need)
    return _moe_ffn(h, kp, aff_t, cb[:, :, :CB_STRIDE].reshape(-1), wgu, wd)


def _final_kernel(x_ref, m_ref, gt_ref, g_ref, o_ref):
    x = x_ref[...] + gt_ref[0] * m_ref[...]
    ms = jnp.mean(x * x, axis=-1, keepdims=True)
    o_ref[...] = (x * lax.rsqrt(ms + EPS)) * g_ref[...]


def _final(x, moe, gt, g, seq):
    n, d = x.shape
    t = FINAL_TILE
    tpb = seq // t
    tok = pl.BlockSpec((t, d), lambda i: (i, 0))
    return pl.pallas_call(
        _final_kernel, out_shape=jax.ShapeDtypeStruct((n, d), F32), grid=(n // t,),
        in_specs=[tok, tok, pl.BlockSpec((1, 1, d), lambda i: (i // tpb, 0, 0)),
                  pl.BlockSpec((1, d), lambda i: (0, 0))],
        out_specs=tok, compiler_params=_cparams(("arbitrary",)), name="final",
    )(x, moe, gt, g.reshape(1, d))


def _trunk(x3, mods, p):
    bsz, seq, d = x3.shape
    n = bsz * seq
    x = x3.reshape(n, d)
    tables = _rope_tables(seq)
    scale = HEAD_DIM ** -0.5
    sh1, sc1, gt1, sh2, sc2, gt2 = [m[:, None, :] for m in jnp.split(mods[0], 6, axis=-1)]
    qw, kw = A_HEADS * HEAD_DIM, A_KV_HEADS * HEAD_DIM
    pieces = ((0, qw, True, scale, 1, False), (qw, kw, True, 1.0, 1, True),
              (qw + kw, kw, False, 1.0, 1, True))
    q, k, v = _qkv(x, None, p["g_mix"][0], sh1, sc1, p["w_qkv_a"], tables, seq, pieces, TOK_TILE_A)
    (o,) = _banded_attention(
        q.reshape(bsz, seq, qw), k.reshape(bsz, seq, 2 * kw), v.reshape(bsz, seq, 2 * kw),
        bsz=bsz, seq_n=seq, dil=1, q_w=qw, k_w=2 * kw, win=A_WINDOW,
        pairs_per_kgroup=A_HEADS // A_KV_HEADS // 2, sink=p["sink_a"], want_lse=False,
        out_dtype=BF16)
    x, h, aff_t = _oproj([o.reshape(n, qw)], None, (1,), x, p["w_o_a"], gt1, p["g_ffn"][0],
                         sh2, sc2, p["w_router"][0], p["b_router"][0], seq, OPROJ_TILE_A)
    moe = _moe(h, aff_t, p["w_gate_up"][0], p["w_down"][0])
    gt2_prev = gt2
    sh1, sc1, gt1, sh2, sc2, gt2 = [m[:, None, :] for m in jnp.split(mods[1], 6, axis=-1)]
    ng = len(B_GROUPS)
    dils = tuple(dil for _, dil in B_GROUPS)
    pieces = tuple((part * ng * B_GW + gi * B_GW, B_GW, part < 2, scale if part == 0 else 1.0, dils[gi], False)
                   for gi in range(ng) for part in range(3))
    res = _qkv(x, (moe, gt2_prev), p["g_mix"][1], sh1, sc1, p["w_qkv_b"], tables, seq, pieces, TOK_TILE)
    x = res[0]
    outs, lses = [], []
    for gi, (window, dil) in enumerate(B_GROUPS):
        sn = seq // dil
        qg, kg, vg = [r.reshape(bsz, sn, dil * B_GW) for r in res[1 + 3 * gi:4 + 3 * gi]]
        og, lg = _banded_attention(
            qg, kg, vg, bsz=bsz, seq_n=sn, dil=dil, q_w=B_GW, k_w=B_GW,
            win=(window // 2) // dil, pairs_per_kgroup=1, sink=None, want_lse=True,
            out_dtype=BF16)
        outs.append(og.reshape(bsz * sn, dil * B_GW))
        lses.append(lg.reshape(bsz * sn, dil * B_GW))
    x, h, aff_t = _oproj(outs, lses, dils, x, p["w_o_b"], gt1, p["g_ffn"][1],
                         sh2, sc2, p["w_router"][1], p["b_router"][1], seq, TOK_TILE)
    moe = _moe(h, aff_t, p["w_gate_up"][1], p["w_down"][1])
    y = _final(x, moe, gt2, p["g_final"], seq)
    return y.reshape(bsz, seq, d)


def kernel(x_prompt, x_sample, c_prompt, c_sample, w_ada, b_ada, g_mix, g_ffn, w_qkv_a, sink_a, w_o_a, w_qkv_b, w_o_b, w_router, b_router, w_gate, w_up, w_down, g_final):
    bp = c_prompt.shape[0]
    bs = c_sample.shape[0]
    rows = -(-(bp + bs) // SUBLANES) * SUBLANES
    c_all = jnp.zeros((rows, D_MODEL), F32).at[:bp].set(c_prompt).at[bp:bp + bs].set(c_sample)
    mods = _ada(c_all, w_ada, b_ada)
    depth = w_ada.shape[0]
    wr_pad = jnp.zeros((depth, D_MODEL, LANES), F32).at[:, :, :N_EXPERTS].set(w_router).astype(BF16)
    br_pad = jnp.zeros((depth, 1, LANES), F32).at[:, 0, :N_EXPERTS].set(b_router)
    p = dict(
        g_mix=g_mix, g_ffn=g_ffn, g_final=g_final, sink_a=sink_a[0],
        w_qkv_a=w_qkv_a[0].astype(BF16), w_o_a=w_o_a[0].astype(BF16),
        w_qkv_b=w_qkv_b[0].astype(BF16), w_o_b=w_o_b[0].astype(BF16),
        w_router=wr_pad, b_router=br_pad,
        w_gate_up=jnp.concatenate([w_gate, w_up], axis=-1).astype(BF16), w_down=w_down.astype(BF16),
    )
    y_prompt = _trunk(x_prompt, mods[:, :bp], p)
    y_sample = _trunk(x_sample, mods[:, bp:bp + bs], p)
    return (y_prompt, y_sample)
```
